```python
import math
import jax, jax.numpy as jnp
from jax import lax
import numpy as np

D_MODEL = 1024
BATCH = 2
SEQ = 8192
DEPTH = 1
DEC_BATCH = 32
DEC_SEQ = 8
PAST_LEN = 16384
PAGE_SIZE = 128

HEAD_DIM = 64
A_HEADS = 8
A_WIDTH = A_HEADS * HEAD_DIM
B_HEADS = 4
B_VDIM = 2 * HEAD_DIM
B_QK = B_HEADS * 2 * HEAD_DIM
B_WIDTH = B_HEADS * B_VDIM
MOBA_BLOCK = 256
MOBA_TOPK = 3
ROT_DIM = HEAD_DIM // 4
ROPE_THETA = 500000.0
D_FF = 4 * D_MODEL
NORM_EPS = 1e-6
SUBLN_EPS = 1e-5
Q_BLOCK = 128
NEG = -1e30
IN_WIDTHS = (A_WIDTH, A_WIDTH, A_WIDTH, B_QK, B_QK, B_WIDTH, D_MODEL, D_MODEL)
IN_COLS = sum(IN_WIDTHS)

kernel_name = 'moba_diffattn_gated_hybrid_step'


def rmsnorm(x, g, eps=NORM_EPS):
    xf = x.astype(jnp.float32)
    y = xf * lax.rsqrt(jnp.mean(xf * xf, axis=-1, keepdims=True) + eps)
    return (y * g.astype(jnp.float32)).astype(x.dtype)


def rope(x, pos):
    half = ROT_DIM // 2
    inv = ROPE_THETA ** (-jnp.arange(half, dtype=jnp.float32) * 2.0 / ROT_DIM)
    ang = pos.astype(jnp.float32)[:, None] * inv[None, :]
    cos = jnp.cos(ang)[None, :, None, :]
    sin = jnp.sin(ang)[None, :, None, :]
    xr = x[..., :ROT_DIM].astype(jnp.float32)
    x1, x2 = xr[..., :half], xr[..., half:]
    rot = jnp.concatenate([x1 * cos - x2 * sin, x2 * cos + x1 * sin], axis=-1).astype(x.dtype)
    return jnp.concatenate([rot, x[..., ROT_DIM:]], axis=-1)


def mixer_inputs(x, norm_g, w_in, b_gate, pos):
    b, s, _ = x.shape
    z = rmsnorm(x, norm_g) @ w_in
    offs = [0]
    for w in IN_WIDTHS:
        offs.append(offs[-1] + w)
    qa, ka, va, qb, kb, vb = [z[..., offs[i]:offs[i + 1]] for i in range(6)]
    gates = jax.nn.sigmoid(z[..., offs[6]:] + b_gate)
    qa = rope(qa.reshape(b, s, A_HEADS, HEAD_DIM), pos)
    ka = rope(ka.reshape(b, s, A_HEADS, HEAD_DIM), pos)
    va = va.reshape(b, s, A_HEADS, HEAD_DIM)
    qb = rope(qb.reshape(b, s, 2 * B_HEADS, HEAD_DIM), pos).reshape(b, s, B_HEADS, 2, HEAD_DIM)
    kb = rope(kb.reshape(b, s, 2 * B_HEADS, HEAD_DIM), pos).reshape(b, s, B_HEADS, 2, HEAD_DIM)
    vb = vb.reshape(b, s, B_HEADS, B_VDIM)
    return qa, ka, va, qb, kb, vb, gates[..., :D_MODEL], gates[..., D_MODEL:]


def moba_select(qc, means, c, k_sel):
    g = jnp.einsum('bhqd,bhnd->bhqn', qc.astype(jnp.float32), means)
    n_full = means.shape[2]
    g = jnp.where(jnp.arange(n_full)[None, :] < c[:, None], g, NEG)
    _, idx = lax.top_k(g, k_sel)
    return idx, idx < c[:, None]


def moba_combine(s_loc, v_loc, s_sel=None, v_sel=None):
    if s_sel is None:
        p = jax.nn.softmax(s_loc, axis=-1)
        return jnp.einsum('bhqk,bhkd->bhqd', p, v_loc.astype(jnp.float32))
    b, h, q, n, blk = s_sel.shape
    p = jax.nn.softmax(jnp.concatenate([s_sel.reshape(b, h, q, n * blk), s_loc], axis=-1), axis=-1)
    p_sel = p[..., :n * blk].reshape(b, h, q, n, blk)
    return (jnp.einsum('bhqnk,bhqnkd->bhqd', p_sel, v_sel.astype(jnp.float32))
            + jnp.einsum('bhqk,bhkd->bhqd', p[..., n * blk:], v_loc.astype(jnp.float32)))


def moba_prompt(q, k, v):
    b, s, h, d = q.shape
    scale = d ** -0.5
    n_blk = -(-s // MOBA_BLOCK)
    n_full = s // MOBA_BLOCK
    k_sel = min(MOBA_TOPK, n_full)
    pad = n_blk * MOBA_BLOCK - s
    kt = jnp.pad(k, ((0, 0), (0, pad), (0, 0), (0, 0))).transpose(0, 2, 1, 3)
    vt = jnp.pad(v, ((0, 0), (0, pad), (0, 0), (0, 0))).transpose(0, 2, 1, 3)
    if k_sel > 0:
        k_blocks = kt[:, :, :n_full * MOBA_BLOCK].reshape(b, h, n_full, MOBA_BLOCK, d)
        v_blocks = vt[:, :, :n_full * MOBA_BLOCK].reshape(b, h, n_full, MOBA_BLOCK, d)
        means = k_blocks.astype(jnp.float32).mean(axis=3)
    bi = jnp.arange(b)[:, None, None, None]
    hi = jnp.arange(h)[None, :, None, None]
    n_qb = s // Q_BLOCK
    qblk = q.transpose(0, 2, 1, 3).reshape(b, h, n_qb, Q_BLOCK, d).transpose(2, 0, 1, 3, 4)

    def step(args):
        i, qc = args
        pos_q = i * Q_BLOCK + jnp.arange(Q_BLOCK)
        c = pos_q // MOBA_BLOCK
        start = (i * Q_BLOCK // MOBA_BLOCK) * MOBA_BLOCK
        k_own = lax.dynamic_slice_in_dim(kt, start, MOBA_BLOCK, axis=2)
        v_own = lax.dynamic_slice_in_dim(vt, start, MOBA_BLOCK, axis=2)
        pos_own = start + jnp.arange(MOBA_BLOCK)
        s_loc = jnp.einsum('bhqd,bhkd->bhqk', qc, k_own, preferred_element_type=jnp.float32) * scale
        mask = (pos_own[None, :] // MOBA_BLOCK == c[:, None]) & (pos_own[None, :] <= pos_q[:, None])
        s_loc = jnp.where(mask, s_loc, NEG)
        if k_sel == 0:
            return moba_combine(s_loc, v_own)
        idx, valid = moba_select(qc, means, c, k_sel)
        k_g = k_blocks[bi, hi, idx]
        v_g = v_blocks[bi, hi, idx]
        s_sel = jnp.einsum('bhqd,bhqnkd->bhqnk', qc, k_g, preferred_element_type=jnp.float32) * scale
        s_sel = jnp.where(valid[..., None], s_sel, NEG)
        return moba_combine(s_loc, v_own, s_sel, v_g)

    o = lax.map(step, (jnp.arange(n_qb), qblk))
    return o.transpose(1, 0, 3, 2, 4).reshape(b, s, h, d)


def moba_sample(q, k_new, v_new, cache_k, cache_v, page_table, l):
    db, t, h, d = q.shape
    scale = d ** -0.5
    n_pages = PAST_LEN // PAGE_SIZE
    ppb = MOBA_BLOCK // PAGE_SIZE
    n_full = PAST_LEN // MOBA_BLOCK
    k_sel = min(MOBA_TOPK, n_full)
    n_loc = min(MOBA_BLOCK, PAST_LEN)
    pos_q = PAST_LEN + jnp.arange(t)
    c = pos_q // MOBA_BLOCK
    qt = q.transpose(0, 2, 1, 3)
    loc_pages = page_table[:, n_pages - n_loc // PAGE_SIZE:]
    k_loc = jnp.concatenate([cache_k[loc_pages, :, l].reshape(db, n_loc, h, d), k_new], axis=1).transpose(0, 2, 1, 3)
    v_loc = jnp.concatenate([cache_v[loc_pages, :, l].reshape(db, n_loc, h, d), v_new], axis=1).transpose(0, 2, 1, 3)
    pos_loc = PAST_LEN - n_loc + jnp.arange(n_loc + t)
    s_loc = jnp.einsum('bhqd,bhkd->bhqk', qt, k_loc, preferred_element_type=jnp.float32) * scale
    mask = (pos_loc[None, :] // MOBA_BLOCK == c[:, None]) & (pos_loc[None, :] <= pos_q[:, None])
    s_loc = jnp.where(mask, s_loc, NEG)
    if k_sel == 0:
        o = moba_combine(s_loc, v_loc)
    else:
        k_full = cache_k[page_table[:, :n_full * ppb], :, l]
        means = k_full.astype(jnp.float32).reshape(db, n_full, MOBA_BLOCK, h, d).mean(axis=2).transpose(0, 2, 1, 3)
        idx, valid = moba_select(qt, means, c, k_sel)
        phys = page_table[jnp.arange(db)[:, None, None, None, None], idx[..., None] * ppb + jnp.arange(ppb)]
        hi = jnp.arange(h)[None, :, None, None, None]
        k_g = cache_k[phys, :, l, hi, :].reshape(db, h, t, k_sel, MOBA_BLOCK, d)
        v_g = cache_v[phys, :, l, hi, :].reshape(db, h, t, k_sel, MOBA_BLOCK, d)
        s_sel = jnp.einsum('bhqd,bhqnkd->bhqnk', qt, k_g, preferred_element_type=jnp.float32) * scale
        s_sel = jnp.where(valid[..., None], s_sel, NEG)
        o = moba_combine(s_loc, v_loc, s_sel, v_g)
    return o.transpose(0, 2, 1, 3)


def diff_lambda(lq1, lk1, lq2, lk2, lam_init):
    f = jnp.float32
    return (jnp.exp(jnp.sum(lq1.astype(f) * lk1.astype(f)))
            - jnp.exp(jnp.sum(lq2.astype(f) * lk2.astype(f))) + lam_init)


def diff_attend(q, pos_q, k, v, pos_k, lam):
    s = jnp.einsum('bqhcd,bkhcd->bhcqk', q, k, preferred_element_type=jnp.float32) * (HEAD_DIM ** -0.5)
    mask = pos_k[None, :] <= pos_q[:, None]
    p = jax.nn.softmax(jnp.where(mask, s, NEG), axis=-1)
    w = p[:, :, 0] - lam * p[:, :, 1]
    return jnp.einsum('bhqk,bkhe->bqhe', w, v.astype(jnp.float32))


def diff_prompt(q, k, v, lam):
    b, s = q.shape[:2]
    n_qb = s // Q_BLOCK
    pos_k = jnp.arange(s)
    qblk = q.reshape(b, n_qb, Q_BLOCK, B_HEADS, 2, HEAD_DIM).swapaxes(0, 1)

    def step(args):
        i, qc = args
        return diff_attend(qc, i * Q_BLOCK + jnp.arange(Q_BLOCK), k, v, pos_k, lam)

    o = lax.map(step, (jnp.arange(n_qb), qblk))
    return o.swapaxes(0, 1).reshape(b, s, B_HEADS, B_VDIM)


def diff_sample(q, k_new, v_new, cache_k, cache_v, page_table, l, lam):
    db, t = q.shape[:2]
    k_past = cache_k[page_table, :, l].reshape(db, PAST_LEN, B_HEADS, 2, HEAD_DIM)
    v_past = cache_v[page_table, :, l].reshape(db, PAST_LEN, B_HEADS, B_VDIM)
    k = jnp.concatenate([k_past, k_new], axis=1)
    v = jnp.concatenate([v_past, v_new], axis=1)
    return diff_attend(q, PAST_LEN + jnp.arange(t), k, v, jnp.arange(PAST_LEN + t), lam)


def merge_branches(x, oa, ob, ga, gb, subln_g, lam_init, w_pa, w_pb, w_o):
    b, s = x.shape[:2]
    oa = oa.reshape(b, s, A_WIDTH).astype(x.dtype)
    ob = (rmsnorm(ob, subln_g, SUBLN_EPS) * (1.0 - lam_init)).reshape(b, s, B_WIDTH).astype(x.dtype)
    return x + (ga * (oa @ w_pa) + gb * (ob @ w_pb)) @ w_o


def channel_mixer(x, norm_g, w_up, w_down):
    return x + jnp.square(jax.nn.relu(rmsnorm(x, norm_g) @ w_up)) @ w_down


def setup_inputs(seed: int = 0) -> dict:
    key = jax.random.key(seed)
    ks = jax.random.split(key, 24)
    f32 = jnp.float32
    n_pages = PAST_LEN // PAGE_SIZE
    n_pool = (DEC_BATCH * n_pages * 5 + 3) // 4

    def nrm(k, shape, scale):
        return jax.random.normal(k, shape, f32) * scale

    page_table = jax.random.permutation(ks[6], n_pool)[:DEC_BATCH * n_pages].reshape(DEC_BATCH, n_pages).astype(jnp.int32)
    return {
        'x_prompt': jax.random.normal(ks[0], (BATCH, SEQ, D_MODEL), f32),
        'x_sample': jax.random.normal(ks[1], (DEC_BATCH, DEC_SEQ, D_MODEL), f32),
        'cache_moba_k': jax.random.normal(ks[2], (n_pool, PAGE_SIZE, DEPTH, A_HEADS, HEAD_DIM), f32),
        'cache_moba_v': jax.random.normal(ks[3], (n_pool, PAGE_SIZE, DEPTH, A_HEADS, HEAD_DIM), f32),
        'cache_diff_k': jax.random.normal(ks[4], (n_pool, PAGE_SIZE, DEPTH, B_HEADS, 2, HEAD_DIM), f32),
        'cache_diff_v': jax.random.normal(ks[5], (n_pool, PAGE_SIZE, DEPTH, B_HEADS, B_VDIM), f32),
        'page_table': page_table,
        'norm_attn': 1.0 + nrm(ks[7], (DEPTH, D_MODEL), 0.02),
        'w_in': nrm(ks[8], (DEPTH, D_MODEL, IN_COLS), D_MODEL ** -0.5),
        'b_gate': nrm(ks[9], (DEPTH, 2 * D_MODEL), 0.1),
        'lambda_q1': nrm(ks[10], (DEPTH, HEAD_DIM), 0.1),
        'lambda_k1': nrm(ks[11], (DEPTH, HEAD_DIM), 0.1),
        'lambda_q2': nrm(ks[12], (DEPTH, HEAD_DIM), 0.1),
        'lambda_k2': nrm(ks[13], (DEPTH, HEAD_DIM), 0.1),
        'subln_g': 1.0 + nrm(ks[14], (DEPTH, B_VDIM), 0.02),
        'w_pa': nrm(ks[15], (DEPTH, A_WIDTH, D_MODEL), A_WIDTH ** -0.5),
        'w_pb': nrm(ks[16], (DEPTH, B_WIDTH, D_MODEL), B_WIDTH ** -0.5),
        'w_o': nrm(ks[17], (DEPTH, D_MODEL, D_MODEL), D_MODEL ** -0.5),
        'norm_mlp': 1.0 + nrm(ks[18], (DEPTH, D_MODEL), 0.02),
        'w_up': nrm(ks[19], (DEPTH, D_MODEL, D_FF), D_MODEL ** -0.5),
        'w_down': nrm(ks[20], (DEPTH, D_FF, D_MODEL), D_FF ** -0.5),
        'norm_final': 1.0 + nrm(ks[21], (D_MODEL,), 0.02),
    }


def reference(x_prompt, x_sample, cache_moba_k, cache_moba_v, cache_diff_k, cache_diff_v, page_table,
              norm_attn, w_in, b_gate, lambda_q1, lambda_k1, lambda_q2, lambda_k2, subln_g,
              w_pa, w_pb, w_o, norm_mlp, w_up, w_down, norm_final):
    pos_p = jnp.arange(x_prompt.shape[1], dtype=jnp.int32)
    pos_s = PAST_LEN + jnp.arange(x_sample.shape[1], dtype=jnp.int32)
    h_p, h_s = x_prompt, x_sample
    kap, vap, kbp, vbp, kas, vas, kbs, vbs = [], [], [], [], [], [], [], []
    for l in range(DEPTH):
        lam_init = 0.8 - 0.6 * math.exp(-0.3 * l)
        lam = diff_lambda(lambda_q1[l], lambda_k1[l], lambda_q2[l], lambda_k2[l], lam_init)
        qa, ka, va, qb, kb, vb, ga, gb = mixer_inputs(h_p, norm_attn[l], w_in[l], b_gate[l], pos_p)
        oa = moba_prompt(qa, ka, va)
        ob = diff_prompt(qb, kb, vb, lam)
        h_p = merge_branches(h_p, oa, ob, ga, gb, subln_g[l], lam_init, w_pa[l], w_pb[l], w_o[l])
        h_p = channel_mixer(h_p, norm_mlp[l], w_up[l], w_down[l])
        kap.append(ka); vap.append(va); kbp.append(kb); vbp.append(vb)
        qa, ka, va, qb, kb, vb, ga, gb = mixer_inputs(h_s, norm_attn[l], w_in[l], b_gate[l], pos_s)
        oa = moba_sample(qa, ka, va, cache_moba_k, cache_moba_v, page_table, l)
        ob = diff_sample(qb, kb, vb, cache_diff_k, cache_diff_v, page_table, l, lam)
        h_s = merge_branches(h_s, oa, ob, ga, gb, subln_g[l], lam_init, w_pa[l], w_pb[l], w_o[l])
        h_s = channel_mixer(h_s, norm_mlp[l], w_up[l], w_down[l])
        kas.append(ka); vas.append(va); kbs.append(kb); vbs.append(vb)
    y_prompt = rmsnorm(h_p, norm_final)
    y_sample = rmsnorm(h_s, norm_final)
    return (y_prompt, y_sample,
            jnp.stack(kap, axis=2), jnp.stack(vap, axis=2), jnp.stack(kbp, axis=2), jnp.stack(vbp, axis=2),
            jnp.stack(kas, axis=2), jnp.stack(vas, axis=2), jnp.stack(kbs, axis=2), jnp.stack(vbs, axis=2))
```

```python
import functools
import math

import jax
import jax.numpy as jnp
from jax import lax
from jax.experimental import pallas as pl
from jax.experimental.pallas import tpu as pltpu

F32 = jnp.float32
BF16 = jnp.bfloat16

D_MODEL = 1024
HEAD_DIM = 64
A_HEADS = 8
B_HEADS = 4
WIDTH = 512
MOBA_BLOCK = 256
MOBA_TOPK = 3
ROT_DIM = HEAD_DIM // 4
ROPE_THETA = 500000.0
D_FF = 4 * D_MODEL
NORM_EPS = 1e-6
SUBLN_EPS = 1e-5
NEG = -1e30
PAGE_SIZE = 128
LAM_INIT = 0.8 - 0.6 * math.exp(-0.3 * 0)
SCALE = HEAD_DIM ** -0.5

LANES = 128
TOK_TILE = 256
ATT_TILE = 256
PAGES_PER_STEP = 16
N_GROUPS = 8
VMEM_LIMIT = 56 * 1024 * 1024

_OFF_QA, _OFF_KA, _OFF_VA, _OFF_QB, _OFF_KB, _OFF_VB, _OFF_G, _OFF_END = (
    0, 512, 1024, 1536, 2048, 2560, 3072, 5120)


def _const_spec(shape):
    nd = len(shape)
    return pl.BlockSpec(shape, lambda *_: (0,) * nd, pipeline_mode=pl.Buffered(1))


def _rope_tables(pos):
    half = ROT_DIM // 2
    inv = ROPE_THETA ** (-jnp.arange(half, dtype=F32) * 2.0 / ROT_DIM)
    ang = pos.astype(F32)[:, None] * inv[None, :]
    cos, sin = jnp.cos(ang), jnp.sin(ang)
    t = pos.shape[0]
    ones = jnp.ones((t, HEAD_DIM - ROT_DIM), F32)
    zeros = jnp.zeros((t, HEAD_DIM - ROT_DIM), F32)
    zh = jnp.zeros((t, half), F32)
    c = jnp.concatenate([cos, cos, ones], axis=1)
    s1 = jnp.concatenate([-sin, zh, zeros], axis=1)
    s2 = jnp.concatenate([zh, sin, zeros], axis=1)
    return tuple(jnp.concatenate([a, a], axis=1) for a in (c, s1, s2))


def _inproj_kernel(x_ref, g_ref, w_ref, bg_ref, c_ref, s1_ref, s2_ref,
                   ka_ref, va_ref, kb_ref, vb_ref,
                   qa_ref, kaug_ref, vab_ref, qb_ref, kbb_ref, vbb_ref,
                   gate_ref, kmean_ref, *, blocks_per_seq):
    x = x_ref[...]
    ms = jnp.mean(x * x, axis=-1, keepdims=True)
    xn = (x * lax.rsqrt(ms + NORM_EPS) * g_ref[...]).astype(BF16)
    c, s1, s2 = c_ref[...], s1_ref[...], s2_ref[...]

    def proj(lo, hi):
        return jnp.dot(xn, w_ref[:, lo:hi], preferred_element_type=F32)

    def rope(z):
        outs = []
        for k in range(z.shape[1] // LANES):
            zc = z[:, k * LANES:(k + 1) * LANES]
            outs.append(zc * c + pltpu.roll(zc, LANES - ROT_DIM // 2, 1) * s1
                        + pltpu.roll(zc, ROT_DIM // 2, 1) * s2)
        return jnp.concatenate(outs, axis=1)

    qa = rope(proj(_OFF_QA, _OFF_KA))
    qa_ref[...] = (qa * SCALE).astype(BF16)

    ka = rope(proj(_OFF_KA, _OFF_VA))
    ka_ref[...] = ka
    kmean_ref[0] = jnp.mean(ka, axis=0, keepdims=True)
    blk = pl.program_id(0) % blocks_per_seq
    lane = lax.broadcasted_iota(jnp.int32, (x.shape[0], LANES), 1)
    onehot = jnp.where((lane == blk) | (lane == blk + 32), 1.0, 0.0).astype(BF16)
    ka_bf = ka.astype(BF16)
    for p in range(WIDTH // LANES):
        kaug_ref[:, 2 * p * LANES:(2 * p + 1) * LANES] = ka_bf[:, p * LANES:(p + 1) * LANES]
        kaug_ref[:, (2 * p + 1) * LANES:(2 * p + 2) * LANES] = onehot

    va = proj(_OFF_VA, _OFF_QB)
    va_ref[...] = va
    vab_ref[...] = va.astype(BF16)

    qb = rope(proj(_OFF_QB, _OFF_KB))
    qb_ref[...] = (qb * SCALE).astype(BF16)

    kb = rope(proj(_OFF_KB, _OFF_VB))
    kb_ref[...] = kb
    kbb_ref[...] = kb.astype(BF16)

    vb = proj(_OFF_VB, _OFF_G)
    vb_ref[...] = vb
    vbb_ref[...] = vb.astype(BF16)

    z = proj(_OFF_G, _OFF_END) + bg_ref[...]
    gate_ref[...] = 1.0 / (1.0 + jnp.exp(-z))


def _inproj(x, norm_g, w_in, b_gate, tables, blocks_per_seq):
    t = x.shape[0]
    nt = t // TOK_TILE
    tab_tiles = tables[0].shape[0] // TOK_TILE
    row = lambda w: pl.BlockSpec((TOK_TILE, w), lambda i: (i, 0))
    tab = pl.BlockSpec((TOK_TILE, LANES), lambda i: (i % tab_tiles, 0))
    f32o = jax.ShapeDtypeStruct((t, WIDTH), F32)
    bf16o = jax.ShapeDtypeStruct((t, WIDTH), BF16)
    return pl.pallas_call(
        functools.partial(_inproj_kernel, blocks_per_seq=blocks_per_seq),
        grid=(nt,),
        in_specs=[row(D_MODEL), _const_spec((1, D_MODEL)), _const_spec((D_MODEL, _OFF_END)),
                  _const_spec((1, 2 * D_MODEL)), tab, tab, tab],
        out_specs=[row(WIDTH)] * 4 + [row(WIDTH), row(2 * WIDTH)] + [row(WIDTH)] * 4
                  + [row(2 * D_MODEL), pl.BlockSpec((1, 1, WIDTH), lambda i: (i, 0, 0))],
        out_shape=[f32o] * 4 + [bf16o, jax.ShapeDtypeStruct((t, 2 * WIDTH), BF16)] + [bf16o] * 4
                  + [jax.ShapeDtypeStruct((t, 2 * D_MODEL), F32),
                     jax.ShapeDtypeStruct((nt, 1, WIDTH), F32)],
        compiler_params=pltpu.CompilerParams(dimension_semantics=("parallel",),
                                             vmem_limit_bytes=VMEM_LIMIT),
        name="inproj",
    )(x, norm_g, w_in, b_gate, *tables)


def _top3_bias(g, valid, forced, seg_masks, idx, axis):
    selected = forced
    big = jnp.float32(1 << 20)
    for seg in seg_masks:
        gh = jnp.where(seg, jnp.where(valid, g, NEG), -jnp.inf)
        for _ in range(MOBA_TOPK):
            mx = jnp.max(gh, axis=axis, keepdims=True)
            first = jnp.min(jnp.where(gh == mx, idx, big), axis=axis, keepdims=True)
            pick = idx == first
            selected = selected | (pick & valid)
            gh = jnp.where(pick, -jnp.inf, gh)
    return jnp.where(selected, 0.0, NEG)


def _pair_attn_kernel(*refs, moba):
    if moba:
        q_ref, k_ref, v_ref, mhi_ref, mlo_ref, o_ref, qs_ref = refs
    else:
        q_ref, k_ref, v_ref, lq1_ref, lk1_ref, lq2_ref, lk2_ref, sg_ref, o_ref, qs_ref = refs
    i = pl.program_id(2)
    tq = ATT_TILE
    q = q_ref[...]
    lane = lax.broadcasted_iota(jnp.int32, (tq, LANES), 1)
    zero = jnp.zeros_like(q)
    qs_ref[0:tq, 0:LANES] = jnp.where(lane < HEAD_DIM, q, zero)
    qs_ref[tq:2 * tq, 0:LANES] = jnp.where(lane >= HEAD_DIM, q, zero)
    if moba:
        g = (jnp.dot(q, mhi_ref[...], preferred_element_type=F32)
             + jnp.dot(q, mlo_ref[...], preferred_element_type=F32))
        blk = lane & 31
        in_range = lane < 2 * 32
        valid = in_range & (blk < i)
        own = in_range & (blk == i)
        segs = [lane < 32, (lane >= 32) & in_range]
        bias = _top3_bias(g, valid, own, segs, lane.astype(F32), 1).astype(BF16)
        qs_ref[0:tq, LANES:2 * LANES] = jnp.where(lane < 32, bias, jnp.zeros_like(bias))
        qs_ref[tq:2 * tq, LANES:2 * LANES] = jnp.where(segs[1], bias, jnp.zeros_like(bias))

    def step(j, carry, diagonal):
        m, l, acc = carry
        start = pl.multiple_of(j * ATT_TILE, ATT_TILE)
        kb = k_ref[pl.ds(start, ATT_TILE), :]
        vb = v_ref[pl.ds(start, ATT_TILE), :]
        s = lax.dot_general(qs_ref[...], kb, (((1,), (1,)), ((), ())),
                            preferred_element_type=F32)
        if diagonal:
            r = lax.broadcasted_iota(jnp.int32, s.shape, 0) & (tq - 1)
            cidx = lax.broadcasted_iota(jnp.int32, s.shape, 1)
            s = jnp.where(cidx <= r, s, NEG)
        m_new = jnp.maximum(m, jnp.max(s, axis=1, keepdims=True))
        alpha = jnp.exp(m - m_new)
        p = jnp.exp(s - m_new)
        l = alpha * l + jnp.sum(p, axis=1, keepdims=True)
        acc = alpha * acc + jnp.dot(p.astype(BF16), vb, preferred_element_type=F32)
        return m_new, l, acc

    init = (jnp.full((2 * tq, 1), -jnp.inf, F32), jnp.zeros((2 * tq, 1), F32),
            jnp.zeros((2 * tq, LANES), F32))
    carry = lax.fori_loop(0, i, functools.partial(step, diagonal=False), init)
    m, l, acc = step(i, carry, True)
    o = acc * (1.0 / l)
    if moba:
        o_ref[...] = jnp.where(lane < HEAD_DIM, o[0:tq], o[tq:2 * tq]).astype(o_ref.dtype)
    else:
        lam = (jnp.exp(jnp.sum(lq1_ref[...] * lk1_ref[...], axis=1, keepdims=True))
               - jnp.exp(jnp.sum(lq2_ref[...] * lk2_ref[...], axis=1, keepdims=True)) + LAM_INIT)
        d = o[0:tq] - lam * o[tq:2 * tq]
        y = d * lax.rsqrt(jnp.mean(d * d, axis=-1, keepdims=True) + SUBLN_EPS) * sg_ref[...]
        o_ref[...] = (y * (1.0 - LAM_INIT)).astype(o_ref.dtype)


def _pair_attn(q, k, v, extras, moba):
    b, s, _ = q.shape
    npair = WIDTH // LANES
    kw = k.shape[2] // npair
    in_specs = [pl.BlockSpec((None, ATT_TILE, LANES), lambda bi, p, i: (bi, i, p)),
                pl.BlockSpec((None, s, kw), lambda bi, p, i: (bi, 0, p)),
                pl.BlockSpec((None, s, LANES), lambda bi, p, i: (bi, 0, p))]
    if moba:
        in_specs += [pl.BlockSpec((None, None, LANES, LANES), lambda bi, p, i: (bi, p, 0, 0))] * 2
    else:
        in_specs += [_const_spec((1, HEAD_DIM))] * 4 + [_const_spec((1, LANES))]
    return pl.pallas_call(
        functools.partial(_pair_attn_kernel, moba=moba),
        grid=(b, npair, s // ATT_TILE),
        in_specs=in_specs,
        out_specs=pl.BlockSpec((None, ATT_TILE, LANES), lambda bi, p, i: (bi, i, p)),
        out_shape=jax.ShapeDtypeStruct((b, s, WIDTH), BF16),
        scratch_shapes=[pltpu.VMEM((2 * ATT_TILE, kw), BF16)],
        compiler_params=pltpu.CompilerParams(
            dimension_semantics=("parallel", "parallel", "arbitrary"),
            vmem_limit_bytes=VMEM_LIMIT),
        name="moba_prompt" if moba else "diff_prompt",
    )(q, k, v, *extras)


def _moba_mean_mats(kmean, b, s):
    nblk = s // MOBA_BLOCK
    mean = kmean.reshape(b, nblk, WIDTH // LANES, 2, HEAD_DIM)
    mt = jnp.transpose(mean, (0, 2, 3, 4, 1))
    z = jnp.zeros((b, WIDTH // LANES, 2, HEAD_DIM, 32), F32).at[..., :nblk].set(mt)
    eye = jnp.eye(2, dtype=F32)
    m = jnp.einsum('bphdn,hg->bphdgn', z, eye).reshape(b, WIDTH // LANES, LANES, 2 * 32)
    m = jnp.pad(m, ((0, 0), (0, 0), (0, 0), (0, LANES - 2 * 32)))
    hi = m.astype(BF16)
    lo = (m - hi.astype(F32)).astype(BF16)
    return hi, lo


def _decode_kernel(*refs, moba, n_pages, t_new):
    pg = PAGES_PER_STEP
    ns = n_pages // pg
    pt_ref, qbd_ref, knew_ref, vnew_ref = refs[:4]
    pos = 4
    if not moba:
        lq1_ref, lk1_ref, lq2_ref, lk2_ref, sg_ref = refs[pos:pos + 5]
        pos += 5
    kpages = refs[pos:pos + pg]
    vpages = refs[pos + pg:pos + 2 * pg]
    pos += 2 * pg
    o_ref = refs[pos]
    sc_ref, stage_ref, acc_ref, m_ref, il_ref = refs[pos + 1:pos + 6]
    if moba:
        mean_ref, bias_ref = refs[pos + 6:pos + 8]
    del pt_ref
    s = pl.program_id(1)
    ppb = MOBA_BLOCK // PAGE_SIZE
    nblk = n_pages // ppb
    lane = lax.broadcasted_iota(jnp.int32, (PAGE_SIZE, LANES), 1)

    @pl.when(s < ns)
    def _key_phase():
        for k in range(pg):
            kp = kpages[k][...]
            stage_ref[k * PAGE_SIZE:(k + 1) * PAGE_SIZE, :] = kp.astype(BF16)
            if moba and k % ppb == ppb - 1:
                tot = jnp.sum(kp, axis=0, keepdims=True)
                for kk in range(1, ppb):
                    tot = tot + jnp.sum(kpages[k - kk][...], axis=0, keepdims=True)
                mean_ref[pl.ds(s * (pg // ppb) + k // ppb, 1), :] = tot * (1.0 / MOBA_BLOCK)
        sc = jnp.dot(stage_ref[...], qbd_ref[...], preferred_element_type=F32)
        sc_ref[pl.ds(s * pg, pg)] = sc.reshape(pg, PAGE_SIZE, LANES)

    @pl.when(s == ns)
    def _softmax_stats():
        sn = jnp.dot(knew_ref[...].astype(BF16), qbd_ref[...], preferred_element_type=F32)
        lane8 = lax.broadcasted_iota(jnp.int32, sn.shape, 1)
        row8 = lax.broadcasted_iota(jnp.int32, sn.shape, 0)
        sn = jnp.where(row8 <= (lane8 & (t_new - 1)), sn, NEG)
        if moba:
            mean = mean_ref[...]
            mhi = mean.astype(BF16)
            mlo = (mean - mhi.astype(F32)).astype(BF16)
            g = (jnp.dot(mhi, qbd_ref[...], preferred_element_type=F32)
                 + jnp.dot(mlo, qbd_ref[...], preferred_element_type=F32))
            ridx = lax.broadcasted_iota(jnp.int32, g.shape, 0)
            true = ridx >= 0
            bias_ref[...] = _top3_bias(g, true, ridx < 0, [true], ridx.astype(F32), 0)

        def block_scores(n):
            blk = sc_ref[pl.ds(n * ppb, ppb)]
            if moba:
                blk = blk + bias_ref[pl.ds(n, 1), :][None]
            return blk

        def max_body(n, m):
            return jnp.maximum(m, jnp.max(block_scores(n), axis=(0, 1))[None, :])

        m = lax.fori_loop(0, nblk, max_body, jnp.max(sn, axis=0, keepdims=True))

        def sum_body(n, l):
            return l + jnp.sum(jnp.exp(block_scores(n) - m[None]), axis=(0, 1))[None, :]

        pn = jnp.exp(sn - m)
        l = lax.fori_loop(0, nblk, sum_body, jnp.sum(pn, axis=0, keepdims=True))
        il = 1.0 / l
        m_ref[...] = m
        il_ref[...] = il
        pad = PAGE_SIZE - t_new
        pn = jnp.concatenate([pn * il, jnp.zeros((pad, LANES), F32)], axis=0)
        vn = jnp.concatenate([vnew_ref[...], jnp.zeros((pad, WIDTH), F32)], axis=0)
        acc_ref[...] = jnp.dot(pn.T.astype(BF16), vn.astype(BF16), preferred_element_type=F32)

    @pl.when(s >= ns)
    def _value_phase():
        base = (s - ns) * pg
        m = m_ref[...]
        il = il_ref[...]
        pts = []
        for k in range(pg):
            stage_ref[k * PAGE_SIZE:(k + 1) * PAGE_SIZE, :] = vpages[k][...].astype(BF16)
            sc = sc_ref[base + k]
            if moba:
                sc = sc + bias_ref[pl.ds((base + k) // ppb, 1), :]
            p = jnp.exp(sc - m) * il
            pts.append(p.T.astype(BF16))
        pt = jnp.concatenate(pts, axis=1)
        acc_ref[...] += jnp.dot(pt, stage_ref[...], preferred_element_type=F32)

    @pl.when(s == 2 * ns - 1)
    def _finalize():
        acc = acc_ref[...]
        lane_o = lax.broadcasted_iota(jnp.int32, (t_new, WIDTH), 1)
        out = jnp.zeros((t_new, WIDTH), F32)
        if moba:
            for gidx in range(N_GROUPS):
                rows = acc[gidx * t_new:(gidx + 1) * t_new]
                out = jnp.where(lane_o // HEAD_DIM == gidx, rows, out)
            o_ref[...] = out.astype(o_ref.dtype)
        else:
            lam = (jnp.exp(jnp.sum(lq1_ref[...] * lk1_ref[...], axis=1, keepdims=True))
                   - jnp.exp(jnp.sum(lq2_ref[...] * lk2_ref[...], axis=1, keepdims=True))
                   + LAM_INIT)
            for h in range(B_HEADS):
                d = (acc[2 * h * t_new:(2 * h + 1) * t_new]
                     - lam * acc[(2 * h + 1) * t_new:(2 * h + 2) * t_new])
                d = d[:, h * LANES:(h + 1) * LANES]
                y = d * lax.rsqrt(jnp.mean(d * d, axis=-1, keepdims=True) + SUBLN_EPS) * sg_ref[...]
                o_ref[:, h * LANES:(h + 1) * LANES] = (y * (1.0 - LAM_INIT)).astype(o_ref.dtype)


def _decode_attn(page_table, qbd, k_new, v_new, cache_k, cache_v, extras, moba):
    db, n_pages = page_table.shape
    t_new = k_new.shape[1]
    pg = PAGES_PER_STEP
    ns = n_pages // pg

    def kmap(k):
        return lambda b, s, pt: (pt[b, jnp.minimum(s, ns - 1) * pg + k], 0, 0)

    def vmap_(k):
        return lambda b, s, pt: (pt[b, jnp.maximum(s - ns, 0) * pg + k], 0, 0)

    page = lambda f: pl.BlockSpec((None, PAGE_SIZE, WIDTH), f)
    per_seq = lambda shape: pl.BlockSpec((None,) + shape, lambda b, s, pt: (b, 0, 0))
    const = lambda shape: pl.BlockSpec(shape, lambda b, s, pt: (0, 0))
    in_specs = [per_seq((WIDTH, LANES)), per_seq((t_new, WIDTH)), per_seq((t_new, WIDTH))]
    if not moba:
        in_specs += [const((1, HEAD_DIM))] * 4 + [const((1, LANES))]
    in_specs += [page(kmap(k)) for k in range(pg)] + [page(vmap_(k)) for k in range(pg)]
    scratch = [pltpu.VMEM((n_pages, PAGE_SIZE, LANES), F32),
               pltpu.VMEM((pg * PAGE_SIZE, WIDTH), BF16),
               pltpu.VMEM((LANES, WIDTH), F32),
               pltpu.VMEM((1, LANES), F32), pltpu.VMEM((1, LANES), F32)]
    if moba:
        nblk = n_pages * PAGE_SIZE // MOBA_BLOCK
        scratch += [pltpu.VMEM((nblk, WIDTH), F32), pltpu.VMEM((nblk, LANES), F32)]
    grid_spec = pltpu.PrefetchScalarGridSpec(
        num_scalar_prefetch=1, grid=(db, 2 * ns), in_specs=in_specs,
        out_specs=pl.BlockSpec((None, t_new, WIDTH), lambda b, s, pt: (b, 0, 0)),
        scratch_shapes=scratch)
    return pl.pallas_call(
        functools.partial(_decode_kernel, moba=moba, n_pages=n_pages, t_new=t_new),
        grid_spec=grid_spec,
        out_shape=jax.ShapeDtypeStruct((db, t_new, WIDTH), F32),
        compiler_params=pltpu.CompilerParams(dimension_semantics=("parallel", "arbitrary"),
                                             vmem_limit_bytes=VMEM_LIMIT),
        name="moba_decode" if moba else "diff_decode",
    )(page_table, qbd, k_new, v_new, *extras, *([cache_k] * pg), *([cache_v] * pg))


def _block_diag_queries(q, db, t_new):
    q4 = jnp.transpose(q.reshape(db, t_new, N_GROUPS, HEAD_DIM), (0, 2, 3, 1))
    eye = jnp.eye(N_GROUPS, dtype=q.dtype)
    qbd = jnp.einsum('bgdt,gh->bgdht', q4, eye).reshape(db, WIDTH, N_GROUPS * t_new)
    return jnp.pad(qbd, ((0, 0), (0, 0), (0, LANES - N_GROUPS * t_new)))


def _merge_mlp_kernel(x_ref, oa_ref, ob_ref, gate_ref, wpa_ref, wpb_ref, wo_ref, gm_ref,
                      wup_ref, wdn_ref, gf_ref, y_ref):
    ga = gate_ref[:, 0:D_MODEL]
    gb = gate_ref[:, D_MODEL:2 * D_MODEL]
    t = (ga * jnp.dot(oa_ref[...], wpa_ref[...], preferred_element_type=F32)
         + gb * jnp.dot(ob_ref[...], wpb_ref[...], preferred_element_type=F32))
    h = x_ref[...] + jnp.dot(t.astype(BF16), wo_ref[...], preferred_element_type=F32)
    hn = (h * lax.rsqrt(jnp.mean(h * h, axis=-1, keepdims=True) + NORM_EPS) * gm_ref[...]).astype(BF16)
    y = h
    chunk = D_FF // 4
    for c in range(D_FF // chunk):
        u = jnp.dot(hn, wup_ref[:, c * chunk:(c + 1) * chunk], preferred_element_type=F32)
        a = jnp.square(jnp.maximum(u, 0.0)).astype(BF16)
        y = y + jnp.dot(a, wdn_ref[c * chunk:(c + 1) * chunk, :], preferred_element_type=F32)
    y_ref[...] = y * lax.rsqrt(jnp.mean(y * y, axis=-1, keepdims=True) + NORM_EPS) * gf_ref[...]


def _merge_mlp(x, oa, ob, gates, w_pa, w_pb, w_o, norm_mlp, w_up, w_down, norm_final):
    t = x.shape[0]
    row = lambda w: pl.BlockSpec((TOK_TILE, w), lambda i: (i, 0))
    return pl.pallas_call(
        _merge_mlp_kernel,
        grid=(t // TOK_TILE,),
        in_specs=[row(D_MODEL), row(WIDTH), row(WIDTH), row(2 * D_MODEL),
                  _const_spec((WIDTH, D_MODEL)), _const_spec((WIDTH, D_MODEL)),
                  _const_spec((D_MODEL, D_MODEL)), _const_spec((1, D_MODEL)),
                  _const_spec((D_MODEL, D_FF)), _const_spec((D_FF, D_MODEL)),
                  _const_spec((1, D_MODEL))],
        out_specs=row(D_MODEL),
        out_shape=jax.ShapeDtypeStruct((t, D_MODEL), F32),
        compiler_params=pltpu.CompilerParams(dimension_semantics=("parallel",),
                                             vmem_limit_bytes=VMEM_LIMIT),
        name="merge_mlp",
    )(x, oa, ob, gates, w_pa, w_pb, w_o, norm_mlp, w_up, w_down, norm_final)


def kernel(x_prompt, x_sample, cache_moba_k, cache_moba_v, cache_diff_k, cache_diff_v, page_table,
           norm_attn, w_in, b_gate, lambda_q1, lambda_k1, lambda_q2, lambda_k2, subln_g,
           w_pa, w_pb, w_o, norm_mlp, w_up, w_down, norm_final):
    b, s, _ = x_prompt.shape
    db, t_new, _ = x_sample.shape
    n_pool = cache_moba_k.shape[0]
    past_len = page_table.shape[1] * PAGE_SIZE
    assert norm_attn.shape[0] == 1 and s % MOBA_BLOCK == 0 and s // MOBA_BLOCK <= 32
    assert (db * t_new) % TOK_TILE == 0 and past_len % MOBA_BLOCK == 0 and t_new == 8
    assert page_table.shape[1] % PAGES_PER_STEP == 0

    w_in_bf = w_in[0, :, :_OFF_END].astype(BF16)
    ng = norm_attn[0][None]
    bg = b_gate[0][None]
    lam_vecs = [v[0][None] for v in (lambda_q1, lambda_k1, lambda_q2, lambda_k2)]
    sg = subln_g[0][None]
    merge_w = (w_pa[0].astype(BF16), w_pb[0].astype(BF16), w_o[0].astype(BF16), norm_mlp[0][None],
               w_up[0].astype(BF16), w_down[0].astype(BF16), norm_final[None])

    xp = x_prompt.reshape(b * s, D_MODEL)
    tabs_p = _rope_tables(jnp.arange(s, dtype=jnp.int32))
    (ka, va, kb, vb, qa_bf, kaug, va_bf, qb_bf, kb_bf, vb_bf, gates_p, kmean) = _inproj(
        xp, ng, w_in_bf, bg, tabs_p, s // MOBA_BLOCK)
    mhi, mlo = _moba_mean_mats(kmean, b, s)
    oa_p = _pair_attn(qa_bf.reshape(b, s, WIDTH), kaug.reshape(b, s, 2 * WIDTH),
                      va_bf.reshape(b, s, WIDTH), (mhi, mlo), moba=True)
    ob_p = _pair_attn(qb_bf.reshape(b, s, WIDTH), kb_bf.reshape(b, s, WIDTH),
                      vb_bf.reshape(b, s, WIDTH), (*lam_vecs, sg), moba=False)
    y_p = _merge_mlp(xp, oa_p.reshape(b * s, WIDTH), ob_p.reshape(b * s, WIDTH), gates_p, *merge_w)

    xs = x_sample.reshape(db * t_new, D_MODEL)
    tabs_s = tuple(jnp.tile(a, (db, 1)) for a in
                   _rope_tables(past_len + jnp.arange(t_new, dtype=jnp.int32)))
    (kas, vas, kbs, vbs, qas_bf, _, _, qbs_bf, _, _, gates_s, _) = _inproj(
        xs, ng, w_in_bf, bg, tabs_s, 1)
    oa_s = _decode_attn(page_table, _block_diag_queries(qas_bf, db, t_new),
                        kas.reshape(db, t_new, WIDTH), vas.reshape(db, t_new, WIDTH),
                        cache_moba_k.reshape(n_pool, PAGE_SIZE, WIDTH),
                        cache_moba_v.reshape(n_pool, PAGE_SIZE, WIDTH), (), moba=True)
    ob_s = _decode_attn(page_table, _block_diag_queries(qbs_bf, db, t_new),
                        kbs.reshape(db, t_new, WIDTH), vbs.reshape(db, t_new, WIDTH),
                        cache_diff_k.reshape(n_pool, PAGE_SIZE, WIDTH),
                        cache_diff_v.reshape(n_pool, PAGE_SIZE, WIDTH), (*lam_vecs, sg), moba=False)
    y_s = _merge_mlp(xs, oa_s.reshape(db * t_new, WIDTH).astype(BF16),
                     ob_s.reshape(db * t_new, WIDTH).astype(BF16), gates_s, *merge_w)

    return (y_p.reshape(b, s, D_MODEL), y_s.reshape(db, t_new, D_MODEL),
            ka.reshape(b, s, 1, A_HEADS, HEAD_DIM), va.reshape(b, s, 1, A_HEADS, HEAD_DIM),
            kb.reshape(b, s, 1, B_HEADS, 2, HEAD_DIM), vb.reshape(b, s, 1, B_HEADS, 2 * HEAD_DIM),
            kas.reshape(db, t_new, 1, A_HEADS, HEAD_DIM), vas.reshape(db, t_new, 1, A_HEADS, HEAD_DIM),
            kbs.reshape(db, t_new, 1, B_HEADS, 2, HEAD_DIM),
            vbs.reshape(db, t_new, 1, B_HEADS, 2 * HEAD_DIM))
```

```python
import functools
import math

import jax
import jax.numpy as jnp
from jax import lax
from jax.experimental import pallas as pl
from jax.experimental.pallas import tpu as pltpu

F32 = jnp.float32
BF16 = jnp.bfloat16

D_MODEL = 1024
HEAD_DIM = 64
A_HEADS = 8
B_HEADS = 4
WIDTH = 512
MOBA_BLOCK = 256
MOBA_TOPK = 3
ROT_DIM = HEAD_DIM // 4
ROPE_THETA = 500000.0
D_FF = 4 * D_MODEL
NORM_EPS = 1e-6
SUBLN_EPS = 1e-5
NEG = -1e30
PAGE_SIZE = 128
LAM_INIT = 0.8 - 0.6 * math.exp(-0.3 * 0)
SCALE = HEAD_DIM ** -0.5
QSCALE = SCALE * math.log2(math.e)

LANES = 128
TOK_TILE = 256
ATT_TILE = 256
PAGES_PER_STEP = 16
N_GROUPS = 8
VMEM_LIMIT = 56 * 1024 * 1024

_OFF_QA, _OFF_KA, _OFF_VA, _OFF_QB, _OFF_KB, _OFF_VB, _OFF_G, _OFF_END = (
    0, 512, 1024, 1536, 2048, 2560, 3072, 5120)


def _const_spec(shape):
    nd = len(shape)
    return pl.BlockSpec(shape, lambda *_: (0,) * nd, pipeline_mode=pl.Buffered(1))


def _rope_tables(pos):
    half = ROT_DIM // 2
    inv = ROPE_THETA ** (-jnp.arange(half, dtype=F32) * 2.0 / ROT_DIM)
    ang = pos.astype(F32)[:, None] * inv[None, :]
    cos, sin = jnp.cos(ang), jnp.sin(ang)
    t = pos.shape[0]
    ones = jnp.ones((t, HEAD_DIM - ROT_DIM), F32)
    zeros = jnp.zeros((t, HEAD_DIM - ROT_DIM), F32)
    zh = jnp.zeros((t, half), F32)
    c = jnp.concatenate([cos, cos, ones], axis=1)
    s1 = jnp.concatenate([-sin, zh, zeros], axis=1)
    s2 = jnp.concatenate([zh, sin, zeros], axis=1)
    return tuple(jnp.concatenate([a, a], axis=1) for a in (c, s1, s2))


def _inproj_kernel(x_ref, g_ref, w_ref, bg_ref, c_ref, s1_ref, s2_ref,
                   ka_ref, va_ref, kb_ref, vb_ref,
                   qa_ref, kaug_ref, vab_ref, qb_ref, kbb_ref, vbb_ref,
                   gate_ref, kmean_ref, *, blocks_per_seq):
    x = x_ref[...]
    ms = jnp.mean(x * x, axis=-1, keepdims=True)
    xn = (x * lax.rsqrt(ms + NORM_EPS) * g_ref[...]).astype(BF16)
    c, s1, s2 = c_ref[...], s1_ref[...], s2_ref[...]

    def proj(lo, hi):
        return jnp.dot(xn, w_ref[:, lo:hi], preferred_element_type=F32)

    def rope(z):
        outs = []
        for k in range(z.shape[1] // LANES):
            zc = z[:, k * LANES:(k + 1) * LANES]
            outs.append(zc * c + pltpu.roll(zc, LANES - ROT_DIM // 2, 1) * s1
                        + pltpu.roll(zc, ROT_DIM // 2, 1) * s2)
        return jnp.concatenate(outs, axis=1)

    qa = rope(proj(_OFF_QA, _OFF_KA))
    qa_ref[...] = (qa * QSCALE).astype(BF16)

    ka = rope(proj(_OFF_KA, _OFF_VA))
    ka_ref[...] = ka
    kmean_ref[0] = jnp.mean(ka, axis=0, keepdims=True)
    blk = pl.program_id(0) % blocks_per_seq
    lane = lax.broadcasted_iota(jnp.int32, (x.shape[0], LANES), 1)
    onehot = jnp.where((lane == blk) | (lane == blk + 32), 1.0, 0.0).astype(BF16)
    ka_bf = ka.astype(BF16)
    for p in range(WIDTH // LANES):
        kaug_ref[:, 2 * p * LANES:(2 * p + 1) * LANES] = ka_bf[:, p * LANES:(p + 1) * LANES]
        kaug_ref[:, (2 * p + 1) * LANES:(2 * p + 2) * LANES] = onehot

    va = proj(_OFF_VA, _OFF_QB)
    va_ref[...] = va
    vab_ref[0] = va.T.astype(BF16)

    qb = rope(proj(_OFF_QB, _OFF_KB))
    qb_ref[...] = (qb * QSCALE).astype(BF16)

    kb = rope(proj(_OFF_KB, _OFF_VB))
    kb_ref[...] = kb
    kbb_ref[...] = kb.astype(BF16)

    vb = proj(_OFF_VB, _OFF_G)
    vb_ref[...] = vb
    vbb_ref[0] = vb.T.astype(BF16)

    z = proj(_OFF_G, _OFF_END) + bg_ref[...]
    gate_ref[...] = 1.0 / (1.0 + jnp.exp(-z))


def _inproj(x, norm_g, w_in, b_gate, tables, blocks_per_seq):
    t = x.shape[0]
    nt = t // TOK_TILE
    tab_tiles = tables[0].shape[0] // TOK_TILE
    row = lambda w: pl.BlockSpec((TOK_TILE, w), lambda i: (i, 0))
    tab = pl.BlockSpec((TOK_TILE, LANES), lambda i: (i % tab_tiles, 0))
    f32o = jax.ShapeDtypeStruct((t, WIDTH), F32)
    bf16o = jax.ShapeDtypeStruct((t, WIDTH), BF16)
    vt_spec = pl.BlockSpec((1, WIDTH, TOK_TILE), lambda i: (i, 0, 0))
    vt_shape = jax.ShapeDtypeStruct((nt, WIDTH, TOK_TILE), BF16)
    return pl.pallas_call(
        functools.partial(_inproj_kernel, blocks_per_seq=blocks_per_seq),
        grid=(nt,),
        in_specs=[row(D_MODEL), _const_spec((1, D_MODEL)), _const_spec((D_MODEL, _OFF_END)),
                  _const_spec((1, 2 * D_MODEL)), tab, tab, tab],
        out_specs=[row(WIDTH)] * 4 + [row(WIDTH), row(2 * WIDTH), vt_spec, row(WIDTH), row(WIDTH),
                                      vt_spec]
                  + [row(2 * D_MODEL), pl.BlockSpec((1, 1, WIDTH), lambda i: (i, 0, 0))],
        out_shape=[f32o] * 4 + [bf16o, jax.ShapeDtypeStruct((t, 2 * WIDTH), BF16), vt_shape, bf16o,
                                bf16o, vt_shape]
                  + [jax.ShapeDtypeStruct((t, 2 * D_MODEL), F32),
                     jax.ShapeDtypeStruct((nt, 1, WIDTH), F32)],
        compiler_params=pltpu.CompilerParams(dimension_semantics=("parallel",),
                                             vmem_limit_bytes=VMEM_LIMIT),
        name="inproj",
    )(x, norm_g, w_in, b_gate, *tables)


def _top3_bias(g, valid, forced, seg_masks, idx, axis):
    selected = forced
    big = jnp.float32(1 << 20)
    for seg in seg_masks:
        gh = jnp.where(seg, jnp.where(valid, g, NEG), -jnp.inf)
        for _ in range(MOBA_TOPK):
            mx = jnp.max(gh, axis=axis, keepdims=True)
            first = jnp.min(jnp.where(gh == mx, idx, big), axis=axis, keepdims=True)
            pick = idx == first
            selected = selected | (pick & valid)
            gh = jnp.where(pick, -jnp.inf, gh)
    return jnp.where(selected, 0.0, NEG)


def _pair_attn_kernel(*refs, moba):
    if moba:
        q_ref, k_ref, v_ref, mhi_ref, mlo_ref, o_ref, qs_ref = refs
    else:
        q_ref, k_ref, v_ref, lq1_ref, lk1_ref, lq2_ref, lk2_ref, sg_ref, o_ref, qs_ref = refs
    i = pl.program_id(2)
    tq = ATT_TILE
    q = q_ref[...]
    lane = lax.broadcasted_iota(jnp.int32, (tq, LANES), 1)
    zero = jnp.zeros_like(q)
    qs_ref[0:tq, 0:LANES] = jnp.where(lane < HEAD_DIM, q, zero)
    qs_ref[tq:2 * tq, 0:LANES] = jnp.where(lane >= HEAD_DIM, q, zero)
    if moba:
        g = (jnp.dot(q, mhi_ref[...], preferred_element_type=F32)
             + jnp.dot(q, mlo_ref[...], preferred_element_type=F32))
        blk = lane & 31
        in_range = lane < 2 * 32
        valid = in_range & (blk < i)
        own = in_range & (blk == i)
        segs = [lane < 32, (lane >= 32) & in_range]
        bias = _top3_bias(g, valid, own, segs, lane.astype(F32), 1).astype(BF16)
        qs_ref[0:tq, LANES:2 * LANES] = jnp.where(lane < 32, bias, jnp.zeros_like(bias))
        qs_ref[tq:2 * tq, LANES:2 * LANES] = jnp.where(segs[1], bias, jnp.zeros_like(bias))

    def scores(j, c):
        start = pl.multiple_of(j * ATT_TILE, ATT_TILE)
        return lax.dot_general(k_ref[pl.ds(start, ATT_TILE), :], qs_ref[c * tq:(c + 1) * tq, :],
                               (((1,), (1,)), ((), ())), preferred_element_type=F32)

    def softmax(st, m, l):
        m_new = jnp.maximum(m, jnp.max(st, axis=0, keepdims=True))
        alpha = jnp.exp2(m - m_new)
        pt = jnp.exp2(st - m_new)
        return m_new, alpha * l + jnp.sum(pt, axis=0, keepdims=True), alpha, pt.astype(BF16)

    def value(jv, alpha, pt, acc):
        return alpha * acc + jnp.dot(v_ref[jv], pt, preferred_element_type=F32)

    def body(j, carry):
        out = []
        for c in range(2):
            st, m, l, alpha_prev, pt_prev, acc = carry[c]
            st_next = scores(j + 1, c)
            m, l, alpha, pt = softmax(st, m, l)
            acc = value(jnp.maximum(j - 1, 0), alpha_prev, pt_prev, acc)
            out.append((st_next, m, l, alpha, pt, acc))
        return tuple(out)

    init = tuple((scores(0, c), jnp.full((1, tq), -jnp.inf, F32), jnp.zeros((1, tq), F32),
                  jnp.ones((1, tq), F32), jnp.zeros((ATT_TILE, tq), BF16),
                  jnp.zeros((LANES, tq), F32)) for c in range(2))
    carry = lax.fori_loop(0, i, body, init)
    kidx = lax.broadcasted_iota(jnp.int32, (ATT_TILE, tq), 0)
    qidx = lax.broadcasted_iota(jnp.int32, (ATT_TILE, tq), 1)
    ots = []
    for c in range(2):
        st, m, l, alpha_prev, pt_prev, acc = carry[c]
        m, l, alpha, pt = softmax(jnp.where(kidx <= qidx, st, NEG), m, l)
        acc = value(jnp.maximum(i - 1, 0), alpha_prev, pt_prev, acc)
        acc = value(i, alpha, pt, acc)
        ots.append(acc * (1.0 / l))
    ot0, ot1 = ots
    if moba:
        ot = jnp.concatenate([ot0[0:HEAD_DIM], ot1[HEAD_DIM:LANES]], axis=0)
        o_ref[...] = ot.T.astype(o_ref.dtype)
    else:
        lam = (jnp.exp(jnp.sum(lq1_ref[...] * lk1_ref[...], axis=1, keepdims=True))
               - jnp.exp(jnp.sum(lq2_ref[...] * lk2_ref[...], axis=1, keepdims=True)) + LAM_INIT)
        d = (ot0 - lam * ot1).T
        y = d * lax.rsqrt(jnp.mean(d * d, axis=-1, keepdims=True) + SUBLN_EPS) * sg_ref[...]
        o_ref[...] = (y * (1.0 - LAM_INIT)).astype(o_ref.dtype)


def _pair_attn(q, k, v, extras, moba):
    b, s, _ = q.shape
    npair = WIDTH // LANES
    kw = k.shape[2] // npair
    nblk = s // ATT_TILE
    in_specs = [pl.BlockSpec((None, ATT_TILE, LANES), lambda bi, p, i: (bi, i, p)),
                pl.BlockSpec((None, s, kw), lambda bi, p, i: (bi, 0, p)),
                pl.BlockSpec((None, nblk, None, LANES, ATT_TILE), lambda bi, p, i: (bi, 0, p, 0, 0))]
    if moba:
        in_specs += [pl.BlockSpec((None, None, LANES, LANES), lambda bi, p, i: (bi, p, 0, 0))] * 2
    else:
        in_specs += [_const_spec((1, HEAD_DIM))] * 4 + [_const_spec((1, LANES))]
    return pl.pallas_call(
        functools.partial(_pair_attn_kernel, moba=moba),
        grid=(b, npair, s // ATT_TILE),
        in_specs=in_specs,
        out_specs=pl.BlockSpec((None, ATT_TILE, LANES), lambda bi, p, i: (bi, i, p)),
        out_shape=jax.ShapeDtypeStruct((b, s, WIDTH), BF16),
        scratch_shapes=[pltpu.VMEM((2 * ATT_TILE, kw), BF16)],
        compiler_params=pltpu.CompilerParams(
            dimension_semantics=("parallel", "parallel", "arbitrary"),
            vmem_limit_bytes=VMEM_LIMIT),
        name="moba_prompt" if moba else "diff_prompt",
    )(q, k, v, *extras)


def _moba_mean_mats(kmean, b, s):
    nblk = s // MOBA_BLOCK
    mean = kmean.reshape(b, nblk, WIDTH // LANES, 2, HEAD_DIM)
    mt = jnp.transpose(mean, (0, 2, 3, 4, 1))
    z = jnp.zeros((b, WIDTH // LANES, 2, HEAD_DIM, 32), F32).at[..., :nblk].set(mt)
    eye = jnp.eye(2, dtype=F32)
    m = jnp.einsum('bphdn,hg->bphdgn', z, eye).reshape(b, WIDTH // LANES, LANES, 2 * 32)
    m = jnp.pad(m, ((0, 0), (0, 0), (0, 0), (0, LANES - 2 * 32)))
    hi = m.astype(BF16)
    lo = (m - hi.astype(F32)).astype(BF16)
    return hi, lo


def _decode_kernel(*refs, moba, n_pages, t_new):
    pg = PAGES_PER_STEP
    ns = n_pages // pg
    step_keys = pg * PAGE_SIZE
    ppb = MOBA_BLOCK // PAGE_SIZE
    bps = pg // ppb
    ncol = N_GROUPS * t_new
    pt_ref, qbd_ref, knew_ref, vnew_ref = refs[:4]
    pos = 4
    if not moba:
        lq1_ref, lk1_ref, lq2_ref, lk2_ref, sg_ref = refs[pos:pos + 5]
        pos += 5
    kpages = refs[pos:pos + pg]
    vpages = refs[pos + pg:pos + 2 * pg]
    pos += 2 * pg
    o_ref = refs[pos]
    sc_ref, kstage_ref, vstage_ref, acc_ref, onew_ref, m_ref, il_ref = refs[pos + 1:pos + 8]
    if moba:
        mean_ref, = refs[pos + 8:pos + 9]
    del pt_ref
    s = pl.program_id(1)
    lane = lax.broadcasted_iota(jnp.int32, (ncol, LANES), 1)
    row = lax.broadcasted_iota(jnp.int32, (ncol, LANES), 0)

    if moba:
        @pl.when(s == 0)
        def _init():
            mean_ref[...] = jnp.zeros_like(mean_ref)

    @pl.when(s < ns)
    def _key_phase():
        for k in range(pg):
            kp = kpages[k][...]
            kstage_ref[:, k * PAGE_SIZE:(k + 1) * PAGE_SIZE] = kp.astype(BF16)
            if moba and k % ppb == ppb - 1:
                tot = kp
                for kk in range(1, ppb):
                    tot = tot + kpages[k - kk][...]
                mean = jnp.sum(tot, axis=1, keepdims=True) * (1.0 / MOBA_BLOCK)
                n = s * bps + k // ppb
                lane_m = lax.broadcasted_iota(jnp.int32, (WIDTH, LANES), 1)
                mean_ref[...] = jnp.where(lane_m == n, mean, mean_ref[...])
        sc_ref[s] = jnp.dot(qbd_ref[...], kstage_ref[...], preferred_element_type=F32)

    @pl.when(s == ns)
    def _softmax_stats():
        pad = PAGE_SIZE - t_new
        kn = jnp.concatenate([knew_ref[...], jnp.zeros((pad, WIDTH), F32)], axis=0).astype(BF16)
        vn = jnp.concatenate([vnew_ref[...], jnp.zeros((pad, WIDTH), F32)], axis=0).astype(BF16)
        sn = lax.dot_general(qbd_ref[...], kn, (((1,), (1,)), ((), ())),
                             preferred_element_type=F32)
        sn = jnp.where((lane < t_new) & (lane <= (row & (t_new - 1))), sn, NEG)
        if moba:
            mean = mean_ref[...]
            mhi = mean.astype(BF16)
            mlo = (mean - mhi.astype(F32)).astype(BF16)
            g = (jnp.dot(qbd_ref[...], mhi, preferred_element_type=F32)
                 + jnp.dot(qbd_ref[...], mlo, preferred_element_type=F32))
            valid = lane < n_pages // ppb
            bias = _top3_bias(g, valid, lane < 0, [lane >= 0], lane.astype(F32), 1)

            def add_bias(st, carry):
                cols = []
                for bi in range(bps):
                    col = jnp.sum(jnp.where(lane == st * bps + bi, bias, 0.0), axis=1, keepdims=True)
                    cols.append(jnp.broadcast_to(col, (ncol, MOBA_BLOCK)))
                sc_ref[st] = sc_ref[st] + jnp.concatenate(cols, axis=1)
                return carry

            lax.fori_loop(0, ns, add_bias, 0)

        def max_body(st, m):
            return jnp.maximum(m, jnp.max(sc_ref[st], axis=1, keepdims=True))

        m = lax.fori_loop(0, ns, max_body, jnp.max(sn, axis=1, keepdims=True))

        def sum_body(st, l):
            return l + jnp.sum(jnp.exp2(sc_ref[st] - m), axis=1, keepdims=True)

        pn = jnp.exp2(sn - m)
        l = lax.fori_loop(0, ns, sum_body, jnp.sum(pn, axis=1, keepdims=True))
        il = 1.0 / l
        m_ref[...] = m
        il_ref[...] = il
        pn = (pn * il).astype(BF16)
        if moba:
            onew_ref[...] = jnp.dot(pn, vn, preferred_element_type=F32)
            acc_ref[...] = jnp.zeros_like(acc_ref)
        else:
            for h in range(B_HEADS):
                onew_ref[h] = jnp.dot(pn[2 * h * t_new:(2 * h + 2) * t_new],
                                      vn[:, h * LANES:(h + 1) * LANES], preferred_element_type=F32)
            acc_ref[...] = jnp.zeros_like(acc_ref)

    @pl.when(s >= ns)
    def _value_phase():
        p = (jnp.exp2(sc_ref[s - ns] - m_ref[...]) * il_ref[...]).astype(BF16)
        if moba:
            for k in range(pg):
                vstage_ref[:, k * PAGE_SIZE:(k + 1) * PAGE_SIZE] = vpages[k][...].astype(BF16)
            acc_ref[...] += lax.dot_general(p, vstage_ref[...], (((1,), (1,)), ((), ())),
                                            preferred_element_type=F32)
        else:
            for k in range(pg):
                for h in range(B_HEADS):
                    vh = vpages[k][pl.ds(h, PAGE_SIZE, stride=B_HEADS), :]
                    vstage_ref[h, k * PAGE_SIZE:(k + 1) * PAGE_SIZE, :] = vh.astype(BF16)
            for h in range(B_HEADS):
                acc_ref[h] += jnp.dot(p[2 * h * t_new:(2 * h + 2) * t_new], vstage_ref[h],
                                      preferred_element_type=F32)

    @pl.when(s == 2 * ns - 1)
    def _finalize():
        if moba:
            tot = acc_ref[...] + onew_ref[...]
            lane_o = lax.broadcasted_iota(jnp.int32, (t_new, WIDTH), 1)
            out = jnp.zeros((t_new, WIDTH), F32)
            for gidx in range(N_GROUPS):
                out = jnp.where(lane_o // HEAD_DIM == gidx, tot[gidx * t_new:(gidx + 1) * t_new], out)
            o_ref[...] = out
        else:
            lam = (jnp.exp(jnp.sum(lq1_ref[...] * lk1_ref[...], axis=1, keepdims=True))
                   - jnp.exp(jnp.sum(lq2_ref[...] * lk2_ref[...], axis=1, keepdims=True))
                   + LAM_INIT)
            for h in range(B_HEADS):
                tot = acc_ref[h] + onew_ref[h]
                d = tot[0:t_new] - lam * tot[t_new:2 * t_new]
                y = d * lax.rsqrt(jnp.mean(d * d, axis=-1, keepdims=True) + SUBLN_EPS) * sg_ref[...]
                o_ref[:, h * LANES:(h + 1) * LANES] = y * (1.0 - LAM_INIT)


def _decode_attn(page_table, qbd, k_new, v_new, cache_k, cache_v, extras, moba):
    db, n_pages = page_table.shape
    t_new = k_new.shape[1]
    pg = PAGES_PER_STEP
    ns = n_pages // pg

    def kmap(k):
        return lambda b, s, pt: (pt[b, jnp.minimum(s, ns - 1) * pg + k], 0, 0)

    def vmap_(k):
        return lambda b, s, pt: (pt[b, jnp.maximum(s - ns, 0) * pg + k], 0, 0)

    ncol = N_GROUPS * t_new
    step_keys = pg * PAGE_SIZE
    page = lambda f: pl.BlockSpec((None, WIDTH, LANES), f)
    per_seq = lambda shape: pl.BlockSpec((None,) + shape, lambda b, s, pt: (b, 0, 0))
    const = lambda shape: pl.BlockSpec(shape, lambda b, s, pt: (0, 0))
    in_specs = [per_seq((ncol, WIDTH)), per_seq((t_new, WIDTH)), per_seq((t_new, WIDTH))]
    if not moba:
        in_specs += [const((1, HEAD_DIM))] * 4 + [const((1, LANES))]
    in_specs += [page(kmap(k)) for k in range(pg)] + [page(vmap_(k)) for k in range(pg)]
    scratch = [pltpu.VMEM((ns, ncol, step_keys), F32),
               pltpu.VMEM((WIDTH, step_keys), BF16)]
    if moba:
        scratch += [pltpu.VMEM((WIDTH, step_keys), BF16),
                    pltpu.VMEM((ncol, WIDTH), F32), pltpu.VMEM((ncol, WIDTH), F32)]
    else:
        scratch += [pltpu.VMEM((B_HEADS, step_keys, LANES), BF16),
                    pltpu.VMEM((B_HEADS, 2 * t_new, LANES), F32),
                    pltpu.VMEM((B_HEADS, 2 * t_new, LANES), F32)]
    scratch += [pltpu.VMEM((ncol, 1), F32), pltpu.VMEM((ncol, 1), F32)]
    if moba:
        scratch += [pltpu.VMEM((WIDTH, LANES), F32)]
    grid_spec = pltpu.PrefetchScalarGridSpec(
        num_scalar_prefetch=1, grid=(db, 2 * ns), in_specs=in_specs,
        out_specs=pl.BlockSpec((None, t_new, WIDTH), lambda b, s, pt: (b, 0, 0)),
        scratch_shapes=scratch)
    return pl.pallas_call(
        functools.partial(_decode_kernel, moba=moba, n_pages=n_pages, t_new=t_new),
        grid_spec=grid_spec,
        out_shape=jax.ShapeDtypeStruct((db, t_new, WIDTH), F32),
        compiler_params=pltpu.CompilerParams(dimension_semantics=("parallel", "arbitrary"),
                                             vmem_limit_bytes=VMEM_LIMIT),
        name="moba_decode" if moba else "diff_decode",
    )(page_table, qbd, k_new, v_new, *extras, *([cache_k] * pg), *([cache_v] * pg))


def _block_diag_queries(q, db, t_new):
    q4 = q.reshape(db, t_new, N_GROUPS, HEAD_DIM)
    eye = jnp.eye(N_GROUPS, dtype=q.dtype)
    return jnp.einsum('btgd,hg->bhtgd', q4, eye).reshape(db, N_GROUPS * t_new, WIDTH)


def _pages_feature_major(cache):
    nd = cache.ndim
    moved = jnp.transpose(cache, (0,) + tuple(range(2, nd)) + (1,))
    return moved.reshape(cache.shape[0], WIDTH, cache.shape[1])


def _merge_mlp_kernel(x_ref, oa_ref, ob_ref, gate_ref, wpa_ref, wpb_ref, wo_ref, gm_ref,
                      wup_ref, wdn_ref, gf_ref, y_ref):
    ga = gate_ref[:, 0:D_MODEL]
    gb = gate_ref[:, D_MODEL:2 * D_MODEL]
    t = (ga * jnp.dot(oa_ref[...], wpa_ref[...], preferred_element_type=F32)
         + gb * jnp.dot(ob_ref[...], wpb_ref[...], preferred_element_type=F32))
    h = x_ref[...] + jnp.dot(t.astype(BF16), wo_ref[...], preferred_element_type=F32)
    hn = (h * lax.rsqrt(jnp.mean(h * h, axis=-1, keepdims=True) + NORM_EPS) * gm_ref[...]).astype(BF16)
    y = h
    chunk = D_FF // 4
    for c in range(D_FF // chunk):
        u = jnp.dot(hn, wup_ref[:, c * chunk:(c + 1) * chunk], preferred_element_type=F32)
        a = jnp.square(jnp.maximum(u, 0.0)).astype(BF16)
        y = y + jnp.dot(a, wdn_ref[c * chunk:(c + 1) * chunk, :], preferred_element_type=F32)
    y_ref[...] = y * lax.rsqrt(jnp.mean(y * y, axis=-1, keepdims=True) + NORM_EPS) * gf_ref[...]


def _merge_mlp(x, oa, ob, gates, w_pa, w_pb, w_o, norm_mlp, w_up, w_down, norm_final):
    t = x.shape[0]
    row = lambda w: pl.BlockSpec((TOK_TILE, w), lambda i: (i, 0))
    return pl.pallas_call(
        _merge_mlp_kernel,
        grid=(t // TOK_TILE,),
        in_specs=[row(D_MODEL), row(WIDTH), row(WIDTH), row(2 * D_MODEL),
                  _const_spec((WIDTH, D_MODEL)), _const_spec((WIDTH, D_MODEL)),
                  _const_spec((D_MODEL, D_MODEL)), _const_spec((1, D_MODEL)),
                  _const_spec((D_MODEL, D_FF)), _const_spec((D_FF, D_MODEL)),
                  _const_spec((1, D_MODEL))],
        out_specs=row(D_MODEL),
        out_shape=jax.ShapeDtypeStruct((t, D_MODEL), F32),
        compiler_params=pltpu.CompilerParams(dimension_semantics=("parallel",),
                                             vmem_limit_bytes=VMEM_LIMIT),
        name="merge_mlp",
    )(x, oa, ob, gates, w_pa, w_pb, w_o, norm_mlp, w_up, w_down, norm_final)


def kernel(x_prompt, x_sample, cache_moba_k, cache_moba_v, cache_diff_k, cache_diff_v, page_table,
           norm_attn, w_in, b_gate, lambda_q1, lambda_k1, lambda_q2, lambda_k2, subln_g,
           w_pa, w_pb, w_o, norm_mlp, w_up, w_down, norm_final):
    b, s, _ = x_prompt.shape
    db, t_new, _ = x_sample.shape
    n_pool = cache_moba_k.shape[0]
    past_len = page_table.shape[1] * PAGE_SIZE
    assert norm_attn.shape[0] == 1 and s % MOBA_BLOCK == 0 and s // MOBA_BLOCK <= 32
    assert (db * t_new) % TOK_TILE == 0 and past_len % MOBA_BLOCK == 0 and t_new == 8
    assert page_table.shape[1] % PAGES_PER_STEP == 0

    w_in_bf = w_in[0, :, :_OFF_END].astype(BF16)
    ng = norm_attn[0][None]
    bg = b_gate[0][None]
    lam_vecs = [v[0][None] for v in (lambda_q1, lambda_k1, lambda_q2, lambda_k2)]
    sg = subln_g[0][None]
    merge_w = (w_pa[0].astype(BF16), w_pb[0].astype(BF16), w_o[0].astype(BF16), norm_mlp[0][None],
               w_up[0].astype(BF16), w_down[0].astype(BF16), norm_final[None])

    xp = x_prompt.reshape(b * s, D_MODEL)
    tabs_p = _rope_tables(jnp.arange(s, dtype=jnp.int32))
    (ka, va, kb, vb, qa_bf, kaug, va_bf, qb_bf, kb_bf, vb_bf, gates_p, kmean) = _inproj(
        xp, ng, w_in_bf, bg, tabs_p, s // MOBA_BLOCK)
    mhi, mlo = _moba_mean_mats(kmean, b, s)
    vt_view = lambda vt: vt.reshape(b, s // ATT_TILE, WIDTH // LANES, LANES, ATT_TILE)
    oa_p = _pair_attn(qa_bf.reshape(b, s, WIDTH), kaug.reshape(b, s, 2 * WIDTH),
                      vt_view(va_bf), (mhi, mlo), moba=True)
    ob_p = _pair_attn(qb_bf.reshape(b, s, WIDTH), kb_bf.reshape(b, s, WIDTH),
                      vt_view(vb_bf), (*lam_vecs, sg), moba=False)
    y_p = _merge_mlp(xp, oa_p.reshape(b * s, WIDTH), ob_p.reshape(b * s, WIDTH), gates_p, *merge_w)

    xs = x_sample.reshape(db * t_new, D_MODEL)
    tabs_s = tuple(jnp.tile(a, (db, 1)) for a in
                   _rope_tables(past_len + jnp.arange(t_new, dtype=jnp.int32)))
    (kas, vas, kbs, vbs, qas_bf, _, _, qbs_bf, _, _, gates_s, _) = _inproj(
        xs, ng, w_in_bf, bg, tabs_s, 1)
    oa_s = _decode_attn(page_table, _block_diag_queries(qas_bf, db, t_new),
                        kas.reshape(db, t_new, WIDTH), vas.reshape(db, t_new, WIDTH),
                        _pages_feature_major(cache_moba_k), _pages_feature_major(cache_moba_v),
                        (), moba=True)
    ob_s = _decode_attn(page_table, _block_diag_queries(qbs_bf, db, t_new),
                        kbs.reshape(db, t_new, WIDTH), vbs.reshape(db, t_new, WIDTH),
                        _pages_feature_major(cache_diff_k),
                        cache_diff_v.reshape(n_pool, PAGE_SIZE * B_HEADS, LANES),
                        (*lam_vecs, sg), moba=False)
    y_s = _merge_mlp(xs, oa_s.reshape(db * t_new, WIDTH).astype(BF16),
                     ob_s.reshape(db * t_new, WIDTH).astype(BF16), gates_s, *merge_w)

    return (y_p.reshape(b, s, D_MODEL), y_s.reshape(db, t_new, D_MODEL),
            ka.reshape(b, s, 1, A_HEADS, HEAD_DIM), va.reshape(b, s, 1, A_HEADS, HEAD_DIM),
            kb.reshape(b, s, 1, B_HEADS, 2, HEAD_DIM), vb.reshape(b, s, 1, B_HEADS, 2 * HEAD_DIM),
            kas.reshape(db, t_new, 1, A_HEADS, HEAD_DIM), vas.reshape(db, t_new, 1, A_HEADS, HEAD_DIM),
            kbs.reshape(db, t_new, 1, B_HEADS, 2, HEAD_DIM),
            vbs.reshape(db, t_new, 1, B_HEADS, 2 * HEAD_DIM))
```

```python
import functools
import math

import jax
import jax.numpy as jnp
from jax import lax
from jax.experimental import pallas as pl
from jax.experimental.pallas import tpu as pltpu

F32 = jnp.float32
BF16 = jnp.bfloat16

D_MODEL = 1024
HEAD_DIM = 64
A_HEADS = 8
B_HEADS = 4
WIDTH = 512
MOBA_BLOCK = 256
MOBA_TOPK = 3
ROT_DIM = HEAD_DIM // 4
ROPE_THETA = 500000.0
D_FF = 4 * D_MODEL
NORM_EPS = 1e-6
SUBLN_EPS = 1e-5
NEG = -1e30
PAGE_SIZE = 128
LAM_INIT = 0.8 - 0.6 * math.exp(-0.3 * 0)
SCALE = HEAD_DIM ** -0.5
QSCALE = SCALE * math.log2(math.e)

LANES = 128
TOK_TILE = 256
ATT_TILE = 256
ATT_PAIRS = 4
PAGES_PER_STEP = 16
N_GROUPS = 8
VMEM_LIMIT = 56 * 1024 * 1024

_OFF_QA, _OFF_KA, _OFF_VA, _OFF_QB, _OFF_KB, _OFF_VB, _OFF_G, _OFF_END = (
    0, 512, 1024, 1536, 2048, 2560, 3072, 5120)


def _const_spec(shape):
    nd = len(shape)
    return pl.BlockSpec(shape, lambda *_: (0,) * nd, pipeline_mode=pl.Buffered(1))


def _rope_tables(pos):
    half = ROT_DIM // 2
    inv = ROPE_THETA ** (-jnp.arange(half, dtype=F32) * 2.0 / ROT_DIM)
    ang = pos.astype(F32)[:, None] * inv[None, :]
    cos, sin = jnp.cos(ang), jnp.sin(ang)
    t = pos.shape[0]
    ones = jnp.ones((t, HEAD_DIM - ROT_DIM), F32)
    zeros = jnp.zeros((t, HEAD_DIM - ROT_DIM), F32)
    zh = jnp.zeros((t, half), F32)
    c = jnp.concatenate([cos, cos, ones], axis=1)
    s1 = jnp.concatenate([-sin, zh, zeros], axis=1)
    s2 = jnp.concatenate([zh, sin, zeros], axis=1)
    return tuple(jnp.concatenate([a, a], axis=1) for a in (c, s1, s2))


def _inproj_kernel(x_ref, g_ref, w_ref, bg_ref, c_ref, s1_ref, s2_ref,
                   ka_ref, va_ref, kb_ref, vb_ref,
                   qa_ref, kaug_ref, vab_ref, qb_ref, kbb_ref, vbb_ref,
                   gate_ref, kmean_ref, *, blocks_per_seq):
    x = x_ref[...]
    ms = jnp.mean(x * x, axis=-1, keepdims=True)
    xn = (x * lax.rsqrt(ms + NORM_EPS) * g_ref[...]).astype(BF16)
    c, s1, s2 = c_ref[...], s1_ref[...], s2_ref[...]

    def proj(lo, hi):
        return jnp.dot(xn, w_ref[:, lo:hi], preferred_element_type=F32)

    def rope(z):
        outs = []
        for k in range(z.shape[1] // LANES):
            zc = z[:, k * LANES:(k + 1) * LANES]
            outs.append(zc * c + pltpu.roll(zc, LANES - ROT_DIM // 2, 1) * s1
                        + pltpu.roll(zc, ROT_DIM // 2, 1) * s2)
        return jnp.concatenate(outs, axis=1)

    qa = rope(proj(_OFF_QA, _OFF_KA))
    qa_ref[...] = (qa * QSCALE).astype(BF16)

    ka = rope(proj(_OFF_KA, _OFF_VA))
    ka_ref[...] = ka
    kmean_ref[0] = jnp.mean(ka, axis=0, keepdims=True)
    blk = pl.program_id(0) % blocks_per_seq
    lane = lax.broadcasted_iota(jnp.int32, (x.shape[0], LANES), 1)
    onehot = jnp.where((lane == blk) | (lane == blk + 32), 1.0, 0.0).astype(BF16)
    ka_bf = ka.astype(BF16)
    for p in range(WIDTH // LANES):
        kaug_ref[:, 2 * p * LANES:(2 * p + 1) * LANES] = ka_bf[:, p * LANES:(p + 1) * LANES]
        kaug_ref[:, (2 * p + 1) * LANES:(2 * p + 2) * LANES] = onehot

    va = proj(_OFF_VA, _OFF_QB)
    va_ref[...] = va
    vab_ref[0] = va.T.astype(BF16)

    qb = rope(proj(_OFF_QB, _OFF_KB))
    qb_ref[...] = (qb * QSCALE).astype(BF16)

    kb = rope(proj(_OFF_KB, _OFF_VB))
    kb_ref[...] = kb
    kbb_ref[...] = kb.astype(BF16)

    vb = proj(_OFF_VB, _OFF_G)
    vb_ref[...] = vb
    vbb_ref[0] = vb.T.astype(BF16)

    z = proj(_OFF_G, _OFF_END) + bg_ref[...]
    gate_ref[...] = 1.0 / (1.0 + jnp.exp(-z))


def _inproj(x, norm_g, w_in, b_gate, tables, blocks_per_seq):
    t = x.shape[0]
    nt = t // TOK_TILE
    tab_tiles = tables[0].shape[0] // TOK_TILE
    row = lambda w: pl.BlockSpec((TOK_TILE, w), lambda i: (i, 0))
    tab = pl.BlockSpec((TOK_TILE, LANES), lambda i: (i % tab_tiles, 0))
    f32o = jax.ShapeDtypeStruct((t, WIDTH), F32)
    bf16o = jax.ShapeDtypeStruct((t, WIDTH), BF16)
    vt_spec = pl.BlockSpec((1, WIDTH, TOK_TILE), lambda i: (i, 0, 0))
    vt_shape = jax.ShapeDtypeStruct((nt, WIDTH, TOK_TILE), BF16)
    return pl.pallas_call(
        functools.partial(_inproj_kernel, blocks_per_seq=blocks_per_seq),
        grid=(nt,),
        in_specs=[row(D_MODEL), _const_spec((1, D_MODEL)), _const_spec((D_MODEL, _OFF_END)),
                  _const_spec((1, 2 * D_MODEL)), tab, tab, tab],
        out_specs=[row(WIDTH)] * 4 + [row(WIDTH), row(2 * WIDTH), vt_spec, row(WIDTH), row(WIDTH),
                                      vt_spec]
                  + [row(2 * D_MODEL), pl.BlockSpec((1, 1, WIDTH), lambda i: (i, 0, 0))],
        out_shape=[f32o] * 4 + [bf16o, jax.ShapeDtypeStruct((t, 2 * WIDTH), BF16), vt_shape, bf16o,
                                bf16o, vt_shape]
                  + [jax.ShapeDtypeStruct((t, 2 * D_MODEL), F32),
                     jax.ShapeDtypeStruct((nt, 1, WIDTH), F32)],
        compiler_params=pltpu.CompilerParams(dimension_semantics=("parallel",),
                                             vmem_limit_bytes=VMEM_LIMIT),
        name="inproj",
    )(x, norm_g, w_in, b_gate, *tables)


def _top3_bias(g, valid, forced, seg_masks, idx, axis):
    selected = forced
    big = jnp.float32(1 << 20)
    for seg in seg_masks:
        gh = jnp.where(seg, jnp.where(valid, g, NEG), -jnp.inf)
        for _ in range(MOBA_TOPK):
            mx = jnp.max(gh, axis=axis, keepdims=True)
            first = jnp.min(jnp.where(gh == mx, idx, big), axis=axis, keepdims=True)
            pick = idx == first
            selected = selected | (pick & valid)
            gh = jnp.where(pick, -jnp.inf, gh)
    return jnp.where(selected, 0.0, NEG)


def _pair_attn_kernel(*refs, moba, pairs):
    qs_ref, s_ref, p_ref, acc_ref, m_ref, l_ref, a_ref = refs[-7:]
    if moba:
        q_ref, k_ref, v_ref, mhi_ref, mlo_ref, o_ref = refs[:-7]
    else:
        q_ref, k_ref, v_ref, lq1_ref, lk1_ref, lq2_ref, lk2_ref, sg_ref, o_ref = refs[:-7]
    i = pl.program_id(2)
    tq = ATT_TILE
    kw = k_ref.shape[1] // pairs
    chains = range(2 * pairs)
    lane = lax.broadcasted_iota(jnp.int32, (tq, LANES), 1)
    for pr in range(pairs):
        q = q_ref[:, pr * LANES:(pr + 1) * LANES]
        zero = jnp.zeros_like(q)
        qs_ref[2 * pr, :, 0:LANES] = jnp.where(lane < HEAD_DIM, q, zero)
        qs_ref[2 * pr + 1, :, 0:LANES] = jnp.where(lane >= HEAD_DIM, q, zero)
        if moba:
            contract_last = (((1,), (1,)), ((), ()))
            g = (lax.dot_general(mhi_ref[pr], q, contract_last, preferred_element_type=F32)
                 + lax.dot_general(mlo_ref[pr], q, contract_last, preferred_element_type=F32))
            ridx = lax.broadcasted_iota(jnp.int32, g.shape, 0)
            blk = ridx & 31
            bias_t = _top3_bias(g, blk < i, blk == i, [ridx < 32, ridx >= 32], ridx.astype(F32), 0)
            bias = jnp.concatenate([bias_t, jnp.zeros_like(bias_t)], axis=0).T.astype(BF16)
            zb = jnp.zeros_like(bias)
            qs_ref[2 * pr, :, LANES:2 * LANES] = jnp.where(lane < 32, bias, zb)
            qs_ref[2 * pr + 1, :, LANES:2 * LANES] = jnp.where((lane >= 32) & (lane < 64), bias, zb)

    def scores(j, ch):
        start = pl.multiple_of(j * ATT_TILE, ATT_TILE)
        kb = k_ref[pl.ds(start, ATT_TILE), (ch // 2) * kw:(ch // 2 + 1) * kw]
        s_ref[ch] = lax.dot_general(kb, qs_ref[ch], (((1,), (1,)), ((), ())),
                                    preferred_element_type=F32)

    def softmax(ch, diagonal):
        st = s_ref[ch]
        if diagonal:
            kidx = lax.broadcasted_iota(jnp.int32, (ATT_TILE, tq), 0)
            qidx = lax.broadcasted_iota(jnp.int32, (ATT_TILE, tq), 1)
            st = jnp.where(kidx <= qidx, st, NEG)
        m = m_ref[ch]
        m_new = jnp.maximum(m, jnp.max(st, axis=0, keepdims=True))
        alpha = jnp.exp2(m - m_new)
        pt = jnp.exp2(st - m_new)
        p_ref[ch] = pt.astype(BF16)
        m_ref[ch] = m_new
        l_ref[ch] = alpha * l_ref[ch] + jnp.sum(pt, axis=0, keepdims=True)
        a_ref[ch] = alpha

    def value(jv, ch):
        acc_ref[ch] = a_ref[ch] * acc_ref[ch] + jnp.dot(v_ref[jv, ch // 2], p_ref[ch],
                                                        preferred_element_type=F32)

    def body(j, carry):
        for ch in chains:
            value(jnp.maximum(j - 1, 0), ch)
            softmax(ch, False)
            scores(j + 1, ch)
        return carry

    for ch in chains:
        scores(0, ch)
        p_ref[ch] = jnp.zeros((ATT_TILE, tq), BF16)
        acc_ref[ch] = jnp.zeros((LANES, tq), F32)
        m_ref[ch] = jnp.full((1, tq), -jnp.inf, F32)
        l_ref[ch] = jnp.zeros((1, tq), F32)
        a_ref[ch] = jnp.ones((1, tq), F32)
    lax.fori_loop(0, i, body, 0)
    if not moba:
        lam = (jnp.exp(jnp.sum(lq1_ref[...] * lk1_ref[...], axis=1, keepdims=True))
               - jnp.exp(jnp.sum(lq2_ref[...] * lk2_ref[...], axis=1, keepdims=True)) + LAM_INIT)
    for pr in range(pairs):
        ots = []
        for ch in (2 * pr, 2 * pr + 1):
            value(jnp.maximum(i - 1, 0), ch)
            softmax(ch, True)
            value(i, ch)
            ots.append(acc_ref[ch] * (1.0 / l_ref[ch]))
        ot0, ot1 = ots
        if moba:
            ot = jnp.concatenate([ot0[0:HEAD_DIM], ot1[HEAD_DIM:LANES]], axis=0)
            o_ref[:, pr * LANES:(pr + 1) * LANES] = ot.T.astype(o_ref.dtype)
        else:
            d = (ot0 - lam * ot1).T
            y = d * lax.rsqrt(jnp.mean(d * d, axis=-1, keepdims=True) + SUBLN_EPS) * sg_ref[...]
            o_ref[:, pr * LANES:(pr + 1) * LANES] = (y * (1.0 - LAM_INIT)).astype(o_ref.dtype)


def _pair_attn(q, k, v, extras, moba):
    b, s, _ = q.shape
    npair = WIDTH // LANES
    kw = k.shape[2] // npair
    nblk = s // ATT_TILE
    pairs = ATT_PAIRS
    nch = 2 * pairs
    in_specs = [pl.BlockSpec((None, ATT_TILE, pairs * LANES), lambda bi, g, i: (bi, i, g)),
                pl.BlockSpec((None, s, pairs * kw), lambda bi, g, i: (bi, 0, g),
                             pipeline_mode=pl.Buffered(1)),
                pl.BlockSpec((None, nblk, pairs, LANES, ATT_TILE), lambda bi, g, i: (bi, 0, g, 0, 0),
                             pipeline_mode=pl.Buffered(1))]
    if moba:
        in_specs += [pl.BlockSpec((None, pairs, 2 * 32, LANES), lambda bi, g, i: (bi, g, 0, 0))] * 2
    else:
        in_specs += [_const_spec((1, HEAD_DIM))] * 4 + [_const_spec((1, LANES))]
    stat = pltpu.VMEM((nch, 1, ATT_TILE), F32)
    return pl.pallas_call(
        functools.partial(_pair_attn_kernel, moba=moba, pairs=pairs),
        grid=(b, npair // pairs, s // ATT_TILE),
        in_specs=in_specs,
        out_specs=pl.BlockSpec((None, ATT_TILE, pairs * LANES), lambda bi, g, i: (bi, i, g)),
        out_shape=jax.ShapeDtypeStruct((b, s, WIDTH), BF16),
        scratch_shapes=[pltpu.VMEM((nch, ATT_TILE, kw), BF16),
                        pltpu.VMEM((nch, ATT_TILE, ATT_TILE), F32),
                        pltpu.VMEM((nch, ATT_TILE, ATT_TILE), BF16),
                        pltpu.VMEM((nch, LANES, ATT_TILE), F32),
                        stat, stat, stat],
        compiler_params=pltpu.CompilerParams(
            dimension_semantics=("parallel", "parallel", "arbitrary"),
            vmem_limit_bytes=VMEM_LIMIT),
        name="moba_prompt" if moba else "diff_prompt",
    )(q, k, v, *extras)


def _moba_mean_mats(kmean, b, s):
    nblk = s // MOBA_BLOCK
    mean = kmean.reshape(b, nblk, WIDTH // LANES, 2, HEAD_DIM)
    mt = jnp.transpose(mean, (0, 2, 3, 1, 4))
    z = jnp.zeros((b, WIDTH // LANES, 2, 32, HEAD_DIM), F32).at[:, :, :, :nblk].set(mt)
    eye = jnp.eye(2, dtype=F32)
    m = jnp.einsum('bphnd,hg->bphngd', z, eye).reshape(b, WIDTH // LANES, 2 * 32, LANES)
    hi = m.astype(BF16)
    lo = (m - hi.astype(F32)).astype(BF16)
    return hi, lo


def _decode_kernel(*refs, moba, n_pages, t_new):
    pg = PAGES_PER_STEP
    ns = n_pages // pg
    step_keys = pg * PAGE_SIZE
    ppb = MOBA_BLOCK // PAGE_SIZE
    bps = pg // ppb
    ncol = N_GROUPS * t_new
    pt_ref, qbd_ref, knew_ref, vnew_ref = refs[:4]
    pos = 4
    if not moba:
        lq1_ref, lk1_ref, lq2_ref, lk2_ref, sg_ref = refs[pos:pos + 5]
        pos += 5
    kpages = refs[pos:pos + pg]
    vpages = refs[pos + pg:pos + 2 * pg]
    pos += 2 * pg
    o_ref = refs[pos]
    sc_ref, kstage_ref, vstage_ref, acc_ref, onew_ref, m_ref, il_ref = refs[pos + 1:pos + 8]
    if moba:
        mean_ref, = refs[pos + 8:pos + 9]
    del pt_ref
    s = pl.program_id(1)
    lane = lax.broadcasted_iota(jnp.int32, (ncol, LANES), 1)
    row = lax.broadcasted_iota(jnp.int32, (ncol, LANES), 0)

    if moba:
        @pl.when(s == 0)
        def _init():
            mean_ref[...] = jnp.zeros_like(mean_ref)

    @pl.when(s < ns)
    def _key_phase():
        for k in range(pg):
            kp = kpages[k][...]
            kstage_ref[:, k * PAGE_SIZE:(k + 1) * PAGE_SIZE] = kp.astype(BF16)
            if moba and k % ppb == ppb - 1:
                tot = kp
                for kk in range(1, ppb):
                    tot = tot + kpages[k - kk][...]
                mean = jnp.sum(tot, axis=1, keepdims=True) * (1.0 / MOBA_BLOCK)
                n = s * bps + k // ppb
                lane_m = lax.broadcasted_iota(jnp.int32, (WIDTH, LANES), 1)
                mean_ref[...] = jnp.where(lane_m == n, mean, mean_ref[...])
        sc_ref[s] = jnp.dot(qbd_ref[...], kstage_ref[...], preferred_element_type=F32)

    @pl.when(s == ns)
    def _softmax_stats():
        pad = PAGE_SIZE - t_new
        kn = jnp.concatenate([knew_ref[...], jnp.zeros((pad, WIDTH), F32)], axis=0).astype(BF16)
        vn = jnp.concatenate([vnew_ref[...], jnp.zeros((pad, WIDTH), F32)], axis=0).astype(BF16)
        sn = lax.dot_general(qbd_ref[...], kn, (((1,), (1,)), ((), ())),
                             preferred_element_type=F32)
        sn = jnp.where((lane < t_new) & (lane <= (row & (t_new - 1))), sn, NEG)
        if moba:
            mean = mean_ref[...]
            mhi = mean.astype(BF16)
            mlo = (mean - mhi.astype(F32)).astype(BF16)
            g = (jnp.dot(qbd_ref[...], mhi, preferred_element_type=F32)
                 + jnp.dot(qbd_ref[...], mlo, preferred_element_type=F32))
            valid = lane < n_pages // ppb
            bias = _top3_bias(g, valid, lane < 0, [lane >= 0], lane.astype(F32), 1)

            def add_bias(st, carry):
                cols = []
                for bi in range(bps):
                    col = jnp.sum(jnp.where(lane == st * bps + bi, bias, 0.0), axis=1, keepdims=True)
                    cols.append(jnp.broadcast_to(col, (ncol, MOBA_BLOCK)))
                sc_ref[st] = sc_ref[st] + jnp.concatenate(cols, axis=1)
                return carry

            lax.fori_loop(0, ns, add_bias, 0)

        def max_body(st, m):
            return jnp.maximum(m, jnp.max(sc_ref[st], axis=1, keepdims=True))

        m = lax.fori_loop(0, ns, max_body, jnp.max(sn, axis=1, keepdims=True))

        def sum_body(st, l):
            return l + jnp.sum(jnp.exp2(sc_ref[st] - m), axis=1, keepdims=True)

        pn = jnp.exp2(sn - m)
        l = lax.fori_loop(0, ns, sum_body, jnp.sum(pn, axis=1, keepdims=True))
        il = 1.0 / l
        m_ref[...] = m
        il_ref[...] = il
        pn = (pn * il).astype(BF16)
        if moba:
            onew_ref[...] = jnp.dot(pn, vn, preferred_element_type=F32)
            acc_ref[...] = jnp.zeros_like(acc_ref)
        else:
            for h in range(B_HEADS):
                onew_ref[h] = jnp.dot(pn[2 * h * t_new:(2 * h + 2) * t_new],
                                      vn[:, h * LANES:(h + 1) * LANES], preferred_element_type=F32)
            acc_ref[...] = jnp.zeros_like(acc_ref)

    @pl.when(s >= ns)
    def _value_phase():
        p = (jnp.exp2(sc_ref[s - ns] - m_ref[...]) * il_ref[...]).astype(BF16)
        if moba:
            for k in range(pg):
                vstage_ref[:, k * PAGE_SIZE:(k + 1) * PAGE_SIZE] = vpages[k][...].astype(BF16)
            acc_ref[...] += lax.dot_general(p, vstage_ref[...], (((1,), (1,)), ((), ())),
                                            preferred_element_type=F32)
        else:
            for k in range(pg):
                for h in range(B_HEADS):
                    vh = vpages[k][pl.ds(h, PAGE_SIZE, stride=B_HEADS), :]
                    vstage_ref[h, k * PAGE_SIZE:(k + 1) * PAGE_SIZE, :] = vh.astype(BF16)
            for h in range(B_HEADS):
                acc_ref[h] += jnp.dot(p[2 * h * t_new:(2 * h + 2) * t_new], vstage_ref[h],
                                      preferred_element_type=F32)

    @pl.when(s == 2 * ns - 1)
    def _finalize():
        if moba:
            tot = acc_ref[...] + onew_ref[...]
            lane_o = lax.broadcasted_iota(jnp.int32, (t_new, WIDTH), 1)
            out = jnp.zeros((t_new, WIDTH), F32)
            for gidx in range(N_GROUPS):
                out = jnp.where(lane_o // HEAD_DIM == gidx, tot[gidx * t_new:(gidx + 1) * t_new], out)
            o_ref[...] = out
        else:
            lam = (jnp.exp(jnp.sum(lq1_ref[...] * lk1_ref[...], axis=1, keepdims=True))
                   - jnp.exp(jnp.sum(lq2_ref[...] * lk2_ref[...], axis=1, keepdims=True))
                   + LAM_INIT)
            for h in range(B_HEADS):
                tot = acc_ref[h] + onew_ref[h]
                d = tot[0:t_new] - lam * tot[t_new:2 * t_new]
                y = d * lax.rsqrt(jnp.mean(d * d, axis=-1, keepdims=True) + SUBLN_EPS) * sg_ref[...]
                o_ref[:, h * LANES:(h + 1) * LANES] = y * (1.0 - LAM_INIT)


def _decode_attn(page_table, qbd, k_new, v_new, cache_k, cache_v, extras, moba):
    db, n_pages = page_table.shape
    t_new = k_new.shape[1]
    pg = PAGES_PER_STEP
    ns = n_pages // pg

    def kmap(k):
        return lambda b, s, pt: (pt[b, jnp.minimum(s, ns - 1) * pg + k], 0, 0)

    def vmap_(k):
        return lambda b, s, pt: (pt[b, jnp.maximum(s - ns, 0) * pg + k], 0, 0)

    ncol = N_GROUPS * t_new
    step_keys = pg * PAGE_SIZE
    page = lambda f: pl.BlockSpec((None, WIDTH, LANES), f)
    per_seq = lambda shape: pl.BlockSpec((None,) + shape, lambda b, s, pt: (b, 0, 0))
    const = lambda shape: pl.BlockSpec(shape, lambda b, s, pt: (0, 0))
    in_specs = [per_seq((ncol, WIDTH)), per_seq((t_new, WIDTH)), per_seq((t_new, WIDTH))]
    if not moba:
        in_specs += [const((1, HEAD_DIM))] * 4 + [const((1, LANES))]
    in_specs += [page(kmap(k)) for k in range(pg)] + [page(vmap_(k)) for k in range(pg)]
    scratch = [pltpu.VMEM((ns, ncol, step_keys), F32),
               pltpu.VMEM((WIDTH, step_keys), BF16)]
    if moba:
        scratch += [pltpu.VMEM((WIDTH, step_keys), BF16),
                    pltpu.VMEM((ncol, WIDTH), F32), pltpu.VMEM((ncol, WIDTH), F32)]
    else:
        scratch += [pltpu.VMEM((B_HEADS, step_keys, LANES), BF16),
                    pltpu.VMEM((B_HEADS, 2 * t_new, LANES), F32),
                    pltpu.VMEM((B_HEADS, 2 * t_new, LANES), F32)]
    scratch += [pltpu.VMEM((ncol, 1), F32), pltpu.VMEM((ncol, 1), F32)]
    if moba:
        scratch += [pltpu.VMEM((WIDTH, LANES), F32)]
    grid_spec = pltpu.PrefetchScalarGridSpec(
        num_scalar_prefetch=1, grid=(db, 2 * ns), in_specs=in_specs,
        out_specs=pl.BlockSpec((None, t_new, WIDTH), lambda b, s, pt: (b, 0, 0)),
        scratch_shapes=scratch)
    return pl.pallas_call(
        functools.partial(_decode_kernel, moba=moba, n_pages=n_pages, t_new=t_new),
        grid_spec=grid_spec,
        out_shape=jax.ShapeDtypeStruct((db, t_new, WIDTH), F32),
        compiler_params=pltpu.CompilerParams(dimension_semantics=("parallel", "arbitrary"),
                                             vmem_limit_bytes=VMEM_LIMIT),
        name="moba_decode" if moba else "diff_decode",
    )(page_table, qbd, k_new, v_new, *extras, *([cache_k] * pg), *([cache_v] * pg))


def _block_diag_queries(q, db, t_new):
    q4 = q.reshape(db, t_new, N_GROUPS, HEAD_DIM)
    eye = jnp.eye(N_GROUPS, dtype=q.dtype)
    return jnp.einsum('btgd,hg->bhtgd', q4, eye).reshape(db, N_GROUPS * t_new, WIDTH)


def _pages_feature_major(cache):
    nd = cache.ndim
    moved = jnp.transpose(cache, (0,) + tuple(range(2, nd)) + (1,))
    return moved.reshape(cache.shape[0], WIDTH, cache.shape[1])


def _merge_mlp_kernel(x_ref, oa_ref, ob_ref, gate_ref, wpa_ref, wpb_ref, wo_ref, gm_ref,
                      wup_ref, wdn_ref, gf_ref, y_ref):
    ga = gate_ref[:, 0:D_MODEL]
    gb = gate_ref[:, D_MODEL:2 * D_MODEL]
    t = (ga * jnp.dot(oa_ref[...], wpa_ref[...], preferred_element_type=F32)
         + gb * jnp.dot(ob_ref[...], wpb_ref[...], preferred_element_type=F32))
    h = x_ref[...] + jnp.dot(t.astype(BF16), wo_ref[...], preferred_element_type=F32)
    hn = (h * lax.rsqrt(jnp.mean(h * h, axis=-1, keepdims=True) + NORM_EPS) * gm_ref[...]).astype(BF16)
    y = h
    chunk = D_FF // 4
    for c in range(D_FF // chunk):
        u = jnp.dot(hn, wup_ref[:, c * chunk:(c + 1) * chunk], preferred_element_type=F32)
        a = jnp.square(jnp.maximum(u, 0.0)).astype(BF16)
        y = y + jnp.dot(a, wdn_ref[c * chunk:(c + 1) * chunk, :], preferred_element_type=F32)
    y_ref[...] = y * lax.rsqrt(jnp.mean(y * y, axis=-1, keepdims=True) + NORM_EPS) * gf_ref[...]


def _merge_mlp(x, oa, ob, gates, w_pa, w_pb, w_o, norm_mlp, w_up, w_down, norm_final):
    t = x.shape[0]
    row = lambda w: pl.BlockSpec((TOK_TILE, w), lambda i: (i, 0))
    return pl.pallas_call(
        _merge_mlp_kernel,
        grid=(t // TOK_TILE,),
        in_specs=[row(D_MODEL), row(WIDTH), row(WIDTH), row(2 * D_MODEL),
                  _const_spec((WIDTH, D_MODEL)), _const_spec((WIDTH, D_MODEL)),
                  _const_spec((D_MODEL, D_MODEL)), _const_spec((1, D_MODEL)),
                  _const_spec((D_MODEL, D_FF)), _const_spec((D_FF, D_MODEL)),
                  _const_spec((1, D_MODEL))],
        out_specs=row(D_MODEL),
        out_shape=jax.ShapeDtypeStruct((t, D_MODEL), F32),
        compiler_params=pltpu.CompilerParams(dimension_semantics=("parallel",),
                                             vmem_limit_bytes=VMEM_LIMIT),
        name="merge_mlp",
    )(x, oa, ob, gates, w_pa, w_pb, w_o, norm_mlp, w_up, w_down, norm_final)


def kernel(x_prompt, x_sample, cache_moba_k, cache_moba_v, cache_diff_k, cache_diff_v, page_table,
           norm_attn, w_in, b_gate, lambda_q1, lambda_k1, lambda_q2, lambda_k2, subln_g,
           w_pa, w_pb, w_o, norm_mlp, w_up, w_down, norm_final):
    b, s, _ = x_prompt.shape
    db, t_new, _ = x_sample.shape
    n_pool = cache_moba_k.shape[0]
    past_len = page_table.shape[1] * PAGE_SIZE
    assert norm_attn.shape[0] == 1 and s % MOBA_BLOCK == 0 and s // MOBA_BLOCK <= 32
    assert (db * t_new) % TOK_TILE == 0 and past_len % MOBA_BLOCK == 0 and t_new == 8
    assert page_table.shape[1] % PAGES_PER_STEP == 0

    w_in_bf = w_in[0, :, :_OFF_END].astype(BF16)
    ng = norm_attn[0][None]
    bg = b_gate[0][None]
    lam_vecs = [v[0][None] for v in (lambda_q1, lambda_k1, lambda_q2, lambda_k2)]
    sg = subln_g[0][None]
    merge_w = (w_pa[0].astype(BF16), w_pb[0].astype(BF16), w_o[0].astype(BF16), norm_mlp[0][None],
               w_up[0].astype(BF16), w_down[0].astype(BF16), norm_final[None])

    xp = x_prompt.reshape(b * s, D_MODEL)
    tabs_p = _rope_tables(jnp.arange(s, dtype=jnp.int32))
    (ka, va, kb, vb, qa_bf, kaug, va_bf, qb_bf, kb_bf, vb_bf, gates_p, kmean) = _inproj(
        xp, ng, w_in_bf, bg, tabs_p, s // MOBA_BLOCK)
    mhi, mlo = _moba_mean_mats(kmean, b, s)
    vt_view = lambda vt: vt.reshape(b, s // ATT_TILE, WIDTH // LANES, LANES, ATT_TILE)
    oa_p = _pair_attn(qa_bf.reshape(b, s, WIDTH), kaug.reshape(b, s, 2 * WIDTH),
                      vt_view(va_bf), (mhi, mlo), moba=True)
    ob_p = _pair_attn(qb_bf.reshape(b, s, WIDTH), kb_bf.reshape(b, s, WIDTH),
                      vt_view(vb_bf), (*lam_vecs, sg), moba=False)
    y_p = _merge_mlp(xp, oa_p.reshape(b * s, WIDTH), ob_p.reshape(b * s, WIDTH), gates_p, *merge_w)

    xs = x_sample.reshape(db * t_new, D_MODEL)
    tabs_s = tuple(jnp.tile(a, (db, 1)) for a in
                   _rope_tables(past_len + jnp.arange(t_new, dtype=jnp.int32)))
    (kas, vas, kbs, vbs, qas_bf, _, _, qbs_bf, _, _, gates_s, _) = _inproj(
        xs, ng, w_in_bf, bg, tabs_s, 1)
    oa_s = _decode_attn(page_table, _block_diag_queries(qas_bf, db, t_new),
                        kas.reshape(db, t_new, WIDTH), vas.reshape(db, t_new, WIDTH),
                        _pages_feature_major(cache_moba_k), _pages_feature_major(cache_moba_v),
                        (), moba=True)
    ob_s = _decode_attn(page_table, _block_diag_queries(qbs_bf, db, t_new),
                        kbs.reshape(db, t_new, WIDTH), vbs.reshape(db, t_new, WIDTH),
                        _pages_feature_major(cache_diff_k),
                        cache_diff_v.reshape(n_pool, PAGE_SIZE * B_HEADS, LANES),
                        (*lam_vecs, sg), moba=False)
    y_s = _merge_mlp(xs, oa_s.reshape(db * t_new, WIDTH).astype(BF16),
                     ob_s.reshape(db * t_new, WIDTH).astype(BF16), gates_s, *merge_w)

    return (y_p.reshape(b, s, D_MODEL), y_s.reshape(db, t_new, D_MODEL),
            ka.reshape(b, s, 1, A_HEADS, HEAD_DIM), va.reshape(b, s, 1, A_HEADS, HEAD_DIM),
            kb.reshape(b, s, 1, B_HEADS, 2, HEAD_DIM), vb.reshape(b, s, 1, B_HEADS, 2 * HEAD_DIM),
            kas.reshape(db, t_new, 1, A_HEADS, HEAD_DIM), vas.reshape(db, t_new, 1, A_HEADS, HEAD_DIM),
            kbs.reshape(db, t_new, 1, B_HEADS, 2, HEAD_DIM),
            vbs.reshape(db, t_new, 1, B_HEADS, 2 * HEAD_DIM))
```

```python
import functools
import math

import jax
import jax.numpy as jnp
from jax import lax
from jax.experimental import pallas as pl
from jax.experimental.pallas import tpu as pltpu

F32 = jnp.float32
BF16 = jnp.bfloat16

D_MODEL = 1024
HEAD_DIM = 64
A_HEADS = 8
B_HEADS = 4
WIDTH = 512
MOBA_BLOCK = 256
MOBA_TOPK = 3
ROT_DIM = HEAD_DIM // 4
ROPE_THETA = 500000.0
D_FF = 4 * D_MODEL
NORM_EPS = 1e-6
SUBLN_EPS = 1e-5
NEG = -1e30
PAGE_SIZE = 128
LAM_INIT = 0.8 - 0.6 * math.exp(-0.3 * 0)
SCALE = HEAD_DIM ** -0.5
QSCALE = SCALE * math.log2(math.e)

LANES = 128
TOK_TILE = 256
ATT_TILE = 256
ATT_PAIRS = 4
PAGES_PER_STEP = 16
PAGE_SLOTS = 3
N_GROUPS = 8
VMEM_LIMIT = 56 * 1024 * 1024

_OFF_QA, _OFF_KA, _OFF_VA, _OFF_QB, _OFF_KB, _OFF_VB, _OFF_G, _OFF_END = (
    0, 512, 1024, 1536, 2048, 2560, 3072, 5120)


def _const_spec(shape):
    nd = len(shape)
    return pl.BlockSpec(shape, lambda *_: (0,) * nd, pipeline_mode=pl.Buffered(1))


def _rope_tables(pos):
    half = ROT_DIM // 2
    inv = ROPE_THETA ** (-jnp.arange(half, dtype=F32) * 2.0 / ROT_DIM)
    ang = pos.astype(F32)[:, None] * inv[None, :]
    cos, sin = jnp.cos(ang), jnp.sin(ang)
    t = pos.shape[0]
    ones = jnp.ones((t, HEAD_DIM - ROT_DIM), F32)
    zeros = jnp.zeros((t, HEAD_DIM - ROT_DIM), F32)
    zh = jnp.zeros((t, half), F32)
    c = jnp.concatenate([cos, cos, ones], axis=1)
    s1 = jnp.concatenate([-sin, zh, zeros], axis=1)
    s2 = jnp.concatenate([zh, sin, zeros], axis=1)
    return tuple(jnp.concatenate([a, a], axis=1) for a in (c, s1, s2))


def _inproj_kernel(x_ref, g_ref, w_ref, bg_ref, c_ref, s1_ref, s2_ref,
                   ka_ref, va_ref, kb_ref, vb_ref,
                   qa_ref, kaug_ref, vab_ref, qb_ref, kbb_ref, vbb_ref,
                   gate_ref, kmean_ref, *, blocks_per_seq, device_layout):
    x = x_ref[...]
    ms = jnp.mean(x * x, axis=-1, keepdims=True)
    xn = (x * lax.rsqrt(ms + NORM_EPS) * g_ref[...]).astype(BF16)
    c, s1, s2 = c_ref[...], s1_ref[...], s2_ref[...]

    def proj(lo, hi):
        return jnp.dot(xn, w_ref[:, lo:hi], preferred_element_type=F32)

    def rope(z):
        outs = []
        for k in range(z.shape[1] // LANES):
            zc = z[:, k * LANES:(k + 1) * LANES]
            outs.append(zc * c + pltpu.roll(zc, LANES - ROT_DIM // 2, 1) * s1
                        + pltpu.roll(zc, ROT_DIM // 2, 1) * s2)
        return jnp.concatenate(outs, axis=1)

    qa = rope(proj(_OFF_QA, _OFF_KA))
    qa_ref[...] = (qa * QSCALE).astype(BF16)

    def store_kv(ref, val, val_t=None):
        if device_layout:
            ref[...] = val.T if val_t is None else val_t
        else:
            ref[...] = val

    ka = rope(proj(_OFF_KA, _OFF_VA))
    store_kv(ka_ref, ka)
    kmean_ref[0] = jnp.mean(ka, axis=0, keepdims=True)
    blk = pl.program_id(0) % blocks_per_seq
    lane = lax.broadcasted_iota(jnp.int32, (x.shape[0], LANES), 1)
    onehot = jnp.where((lane == blk) | (lane == blk + 32), 1.0, 0.0).astype(BF16)
    ka_bf = ka.astype(BF16)
    for p in range(WIDTH // LANES):
        kaug_ref[:, 2 * p * LANES:(2 * p + 1) * LANES] = ka_bf[:, p * LANES:(p + 1) * LANES]
        kaug_ref[:, (2 * p + 1) * LANES:(2 * p + 2) * LANES] = onehot

    va = proj(_OFF_VA, _OFF_QB)
    va_t = va.T
    store_kv(va_ref, va, va_t)
    vab_ref[0] = va_t.astype(BF16)

    qb = rope(proj(_OFF_QB, _OFF_KB))
    qb_ref[...] = (qb * QSCALE).astype(BF16)

    kb = rope(proj(_OFF_KB, _OFF_VB))
    store_kv(kb_ref, kb)
    kbb_ref[...] = kb.astype(BF16)

    vb = proj(_OFF_VB, _OFF_G)
    if device_layout:
        for h in range(B_HEADS):
            vb_ref[pl.ds(h, x.shape[0], stride=B_HEADS), :] = vb[:, h * LANES:(h + 1) * LANES]
    else:
        vb_ref[...] = vb
    vbb_ref[0] = vb.T.astype(BF16)

    z = proj(_OFF_G, _OFF_END) + bg_ref[...]
    gate_ref[...] = 1.0 / (1.0 + jnp.exp(-z))


def _inproj(x, norm_g, w_in, b_gate, tables, blocks_per_seq, device_layout):
    t = x.shape[0]
    nt = t // TOK_TILE
    tab_tiles = tables[0].shape[0] // TOK_TILE
    row = lambda w: pl.BlockSpec((TOK_TILE, w), lambda i: (i, 0))
    tab = pl.BlockSpec((TOK_TILE, LANES), lambda i: (i % tab_tiles, 0))
    bf16o = jax.ShapeDtypeStruct((t, WIDTH), BF16)
    vt_spec = pl.BlockSpec((1, WIDTH, TOK_TILE), lambda i: (i, 0, 0))
    vt_shape = jax.ShapeDtypeStruct((nt, WIDTH, TOK_TILE), BF16)
    if device_layout:
        nb = nt // blocks_per_seq
        kv_spec = pl.BlockSpec((None, WIDTH, TOK_TILE),
                               lambda i: (i // blocks_per_seq, 0, i % blocks_per_seq))
        kv_shape = jax.ShapeDtypeStruct((nb, WIDTH, blocks_per_seq * TOK_TILE), F32)
        f32_specs = [kv_spec] * 3 + [pl.BlockSpec((TOK_TILE * B_HEADS, LANES), lambda i: (i, 0))]
        f32_shapes = [kv_shape] * 3 + [jax.ShapeDtypeStruct((t * B_HEADS, LANES), F32)]
    else:
        f32_specs = [row(WIDTH)] * 4
        f32_shapes = [jax.ShapeDtypeStruct((t, WIDTH), F32)] * 4
    return pl.pallas_call(
        functools.partial(_inproj_kernel, blocks_per_seq=blocks_per_seq,
                          device_layout=device_layout),
        grid=(nt,),
        in_specs=[row(D_MODEL), _const_spec((1, D_MODEL)), _const_spec((D_MODEL, _OFF_END)),
                  _const_spec((1, 2 * D_MODEL)), tab, tab, tab],
        out_specs=f32_specs + [row(WIDTH), row(2 * WIDTH), vt_spec, row(WIDTH), row(WIDTH), vt_spec]
                  + [row(2 * D_MODEL), pl.BlockSpec((1, 1, WIDTH), lambda i: (i, 0, 0))],
        out_shape=f32_shapes + [bf16o, jax.ShapeDtypeStruct((t, 2 * WIDTH), BF16), vt_shape, bf16o,
                                bf16o, vt_shape]
                  + [jax.ShapeDtypeStruct((t, 2 * D_MODEL), F32),
                     jax.ShapeDtypeStruct((nt, 1, WIDTH), F32)],
        compiler_params=pltpu.CompilerParams(dimension_semantics=("parallel",),
                                             vmem_limit_bytes=VMEM_LIMIT),
        name="inproj",
    )(x, norm_g, w_in, b_gate, *tables)


def _top3_bias(g, valid, forced, seg_masks, idx, axis):
    selected = forced
    big = jnp.float32(1 << 20)
    for seg in seg_masks:
        gh = jnp.where(seg, jnp.where(valid, g, NEG), -jnp.inf)
        for _ in range(MOBA_TOPK):
            mx = jnp.max(gh, axis=axis, keepdims=True)
            first = jnp.min(jnp.where(gh == mx, idx, big), axis=axis, keepdims=True)
            pick = idx == first
            selected = selected | (pick & valid)
            gh = jnp.where(pick, -jnp.inf, gh)
    return jnp.where(selected, 0.0, NEG)


def _pair_attn_kernel(*refs, moba, pairs):
    qs_ref, s_ref, p_ref, acc_ref, m_ref, l_ref, a_ref = refs[-7:]
    if moba:
        q_ref, k_ref, v_ref, mhi_ref, mlo_ref, o_ref = refs[:-7]
    else:
        q_ref, k_ref, v_ref, lq1_ref, lk1_ref, lq2_ref, lk2_ref, sg_ref, o_ref = refs[:-7]
    i = pl.program_id(2)
    tq = ATT_TILE
    kw = k_ref.shape[1] // pairs
    chains = range(2 * pairs)
    lane = lax.broadcasted_iota(jnp.int32, (tq, LANES), 1)
    for pr in range(pairs):
        q = q_ref[:, pr * LANES:(pr + 1) * LANES]
        zero = jnp.zeros_like(q)
        qs_ref[2 * pr, :, 0:LANES] = jnp.where(lane < HEAD_DIM, q, zero)
        qs_ref[2 * pr + 1, :, 0:LANES] = jnp.where(lane >= HEAD_DIM, q, zero)
        if moba:
            contract_last = (((1,), (1,)), ((), ()))
            g = (lax.dot_general(mhi_ref[pr], q, contract_last, preferred_element_type=F32)
                 + lax.dot_general(mlo_ref[pr], q, contract_last, preferred_element_type=F32))
            ridx = lax.broadcasted_iota(jnp.int32, g.shape, 0)
            blk = ridx & 31
            bias_t = _top3_bias(g, blk < i, blk == i, [ridx < 32, ridx >= 32], ridx.astype(F32), 0)
            bias = jnp.concatenate([bias_t, jnp.zeros_like(bias_t)], axis=0).T.astype(BF16)
            zb = jnp.zeros_like(bias)
            qs_ref[2 * pr, :, LANES:2 * LANES] = jnp.where(lane < 32, bias, zb)
            qs_ref[2 * pr + 1, :, LANES:2 * LANES] = jnp.where((lane >= 32) & (lane < 64), bias, zb)

    def scores(j, ch):
        start = pl.multiple_of(j * ATT_TILE, ATT_TILE)
        kb = k_ref[pl.ds(start, ATT_TILE), (ch // 2) * kw:(ch // 2 + 1) * kw]
        s_ref[ch] = lax.dot_general(kb, qs_ref[ch], (((1,), (1,)), ((), ())),
                                    preferred_element_type=F32)

    def softmax(ch, diagonal):
        st = s_ref[ch]
        if diagonal:
            kidx = lax.broadcasted_iota(jnp.int32, (ATT_TILE, tq), 0)
            qidx = lax.broadcasted_iota(jnp.int32, (ATT_TILE, tq), 1)
            st = jnp.where(kidx <= qidx, st, NEG)
        m = m_ref[ch]
        m_new = jnp.maximum(m, jnp.max(st, axis=0, keepdims=True))
        alpha = jnp.exp2(m - m_new)
        pt = jnp.exp2(st - m_new)
        p_ref[ch] = pt.astype(BF16)
        m_ref[ch] = m_new
        l_ref[ch] = alpha * l_ref[ch] + jnp.sum(pt, axis=0, keepdims=True)
        a_ref[ch] = alpha

    def value(jv, ch):
        acc_ref[ch] = a_ref[ch] * acc_ref[ch] + jnp.dot(v_ref[jv, ch // 2], p_ref[ch],
                                                        preferred_element_type=F32)

    def body(j, carry):
        for ch in chains:
            value(jnp.maximum(j - 1, 0), ch)
            softmax(ch, False)
            scores(j + 1, ch)
        return carry

    for ch in chains:
        scores(0, ch)
        p_ref[ch] = jnp.zeros((ATT_TILE, tq), BF16)
        acc_ref[ch] = jnp.zeros((LANES, tq), F32)
        m_ref[ch] = jnp.full((1, tq), -jnp.inf, F32)
        l_ref[ch] = jnp.zeros((1, tq), F32)
        a_ref[ch] = jnp.ones((1, tq), F32)
    lax.fori_loop(0, i, body, 0)
    if not moba:
        lam = (jnp.exp(jnp.sum(lq1_ref[...] * lk1_ref[...], axis=1, keepdims=True))
               - jnp.exp(jnp.sum(lq2_ref[...] * lk2_ref[...], axis=1, keepdims=True)) + LAM_INIT)
    for pr in range(pairs):
        ots = []
        for ch in (2 * pr, 2 * pr + 1):
            value(jnp.maximum(i - 1, 0), ch)
            softmax(ch, True)
            value(i, ch)
            ots.append(acc_ref[ch] * (1.0 / l_ref[ch]))
        ot0, ot1 = ots
        if moba:
            ot = jnp.concatenate([ot0[0:HEAD_DIM], ot1[HEAD_DIM:LANES]], axis=0)
            o_ref[:, pr * LANES:(pr + 1) * LANES] = ot.T.astype(o_ref.dtype)
        else:
            d = (ot0 - lam * ot1).T
            y = d * lax.rsqrt(jnp.mean(d * d, axis=-1, keepdims=True) + SUBLN_EPS) * sg_ref[...]
            o_ref[:, pr * LANES:(pr + 1) * LANES] = (y * (1.0 - LAM_INIT)).astype(o_ref.dtype)


def _pair_attn(q, k, v, extras, moba):
    b, s, _ = q.shape
    npair = WIDTH // LANES
    kw = k.shape[2] // npair
    nblk = s // ATT_TILE
    pairs = ATT_PAIRS
    nch = 2 * pairs
    in_specs = [pl.BlockSpec((None, ATT_TILE, pairs * LANES), lambda bi, g, i: (bi, i, g)),
                pl.BlockSpec((None, s, pairs * kw), lambda bi, g, i: (bi, 0, g),
                             pipeline_mode=pl.Buffered(1)),
                pl.BlockSpec((None, nblk, pairs, LANES, ATT_TILE), lambda bi, g, i: (bi, 0, g, 0, 0),
                             pipeline_mode=pl.Buffered(1))]
    if moba:
        in_specs += [pl.BlockSpec((None, pairs, 2 * 32, LANES), lambda bi, g, i: (bi, g, 0, 0))] * 2
    else:
        in_specs += [_const_spec((1, HEAD_DIM))] * 4 + [_const_spec((1, LANES))]
    stat = pltpu.VMEM((nch, 1, ATT_TILE), F32)
    return pl.pallas_call(
        functools.partial(_pair_attn_kernel, moba=moba, pairs=pairs),
        grid=(b, npair // pairs, s // ATT_TILE),
        in_specs=in_specs,
        out_specs=pl.BlockSpec((None, ATT_TILE, pairs * LANES), lambda bi, g, i: (bi, i, g)),
        out_shape=jax.ShapeDtypeStruct((b, s, WIDTH), BF16),
        scratch_shapes=[pltpu.VMEM((nch, ATT_TILE, kw), BF16),
                        pltpu.VMEM((nch, ATT_TILE, ATT_TILE), F32),
                        pltpu.VMEM((nch, ATT_TILE, ATT_TILE), BF16),
                        pltpu.VMEM((nch, LANES, ATT_TILE), F32),
                        stat, stat, stat],
        compiler_params=pltpu.CompilerParams(
            dimension_semantics=("parallel", "parallel", "arbitrary"),
            vmem_limit_bytes=VMEM_LIMIT),
        name="moba_prompt" if moba else "diff_prompt",
    )(q, k, v, *extras)


def _moba_mean_mats(kmean, b, s):
    nblk = s // MOBA_BLOCK
    mean = kmean.reshape(b, nblk, WIDTH // LANES, 2, HEAD_DIM)
    mt = jnp.transpose(mean, (0, 2, 3, 1, 4))
    z = jnp.zeros((b, WIDTH // LANES, 2, 32, HEAD_DIM), F32).at[:, :, :, :nblk].set(mt)
    eye = jnp.eye(2, dtype=F32)
    m = jnp.einsum('bphnd,hg->bphngd', z, eye).reshape(b, WIDTH // LANES, 2 * 32, LANES)
    hi = m.astype(BF16)
    lo = (m - hi.astype(F32)).astype(BF16)
    return hi, lo


def _decode_kernel(*refs, moba, n_pages, t_new):
    pg = PAGES_PER_STEP
    ns = n_pages // pg
    step_keys = pg * PAGE_SIZE
    ppb = MOBA_BLOCK // PAGE_SIZE
    bps = pg // ppb
    ncol = N_GROUPS * t_new
    pt_ref, qbd_ref, knew_ref, vnew_ref = refs[:4]
    pos = 4
    if not moba:
        lq1_ref, lk1_ref, lq2_ref, lk2_ref, sg_ref = refs[pos:pos + 5]
        pos += 5
    ck_ref, cv_ref, o_ref, ring_ref, sem_ref = refs[pos:pos + 5]
    sc_ref, kstage_ref, vstage_ref, acc_ref, onew_ref, m_ref, il_ref = refs[pos + 5:pos + 12]
    if moba:
        mean_ref, = refs[pos + 12:pos + 13]
    b = pl.program_id(0)
    s = pl.program_id(1)
    lane = lax.broadcasted_iota(jnp.int32, (ncol, LANES), 1)
    row = lax.broadcasted_iota(jnp.int32, (ncol, LANES), 0)

    steps = 2 * ns
    g = b * steps + s
    total = pl.num_programs(0) * steps

    def page_copy(cache_ref, page, slot, k):
        return pltpu.make_async_copy(cache_ref.at[page], ring_ref.at[slot, k], sem_ref.at[slot])

    def start_batch(gb):
        bb = gb // steps
        sb = gb - bb * steps
        slot = gb % PAGE_SLOTS

        @pl.when(sb < ns)
        def _():
            for k in range(pg):
                page_copy(ck_ref, pt_ref[bb, sb * pg + k], slot, k).start()

        @pl.when(sb >= ns)
        def _():
            for k in range(pg):
                page_copy(cv_ref, pt_ref[bb, (sb - ns) * pg + k], slot, k).start()

    @pl.when(g == 0)
    def _prime():
        for ahead in range(PAGE_SLOTS - 1):
            start_batch(g + ahead)

    @pl.when(g + PAGE_SLOTS - 1 < total)
    def _prefetch():
        start_batch(g + PAGE_SLOTS - 1)

    slot = g % PAGE_SLOTS
    for k in range(pg):
        page_copy(ck_ref, 0, slot, k).wait()
    kpages = [ring_ref.at[slot, k] for k in range(pg)]
    vpages = kpages

    if moba:
        @pl.when(s == 0)
        def _init():
            mean_ref[...] = jnp.zeros_like(mean_ref)

    @pl.when(s < ns)
    def _key_phase():
        for k in range(pg):
            kp = kpages[k][...]
            kstage_ref[:, k * PAGE_SIZE:(k + 1) * PAGE_SIZE] = kp.astype(BF16)
            if moba and k % ppb == ppb - 1:
                tot = kp
                for kk in range(1, ppb):
                    tot = tot + kpages[k - kk][...]
                mean = jnp.sum(tot, axis=1, keepdims=True) * (1.0 / MOBA_BLOCK)
                n = s * bps + k // ppb
                lane_m = lax.broadcasted_iota(jnp.int32, (WIDTH, LANES), 1)
                mean_ref[...] = jnp.where(lane_m == n, mean, mean_ref[...])
        sc_ref[s] = jnp.dot(qbd_ref[...], kstage_ref[...], preferred_element_type=F32)

    @pl.when(s == ns)
    def _softmax_stats():
        pad = PAGE_SIZE - t_new
        kn = jnp.concatenate([knew_ref[...], jnp.zeros((pad, WIDTH), F32)], axis=0).astype(BF16)
        vn = jnp.concatenate([vnew_ref[...], jnp.zeros((pad, WIDTH), F32)], axis=0).astype(BF16)
        sn = lax.dot_general(qbd_ref[...], kn, (((1,), (1,)), ((), ())),
                             preferred_element_type=F32)
        sn = jnp.where((lane < t_new) & (lane <= (row & (t_new - 1))), sn, NEG)
        if moba:
            mean = mean_ref[...]
            mhi = mean.astype(BF16)
            mlo = (mean - mhi.astype(F32)).astype(BF16)
            g = (jnp.dot(qbd_ref[...], mhi, preferred_element_type=F32)
                 + jnp.dot(qbd_ref[...], mlo, preferred_element_type=F32))
            valid = lane < n_pages // ppb
            bias = _top3_bias(g, valid, lane < 0, [lane >= 0], lane.astype(F32), 1)

            def add_bias(st, carry):
                cols = []
                for bi in range(bps):
                    col = jnp.sum(jnp.where(lane == st * bps + bi, bias, 0.0), axis=1, keepdims=True)
                    cols.append(jnp.broadcast_to(col, (ncol, MOBA_BLOCK)))
                sc_ref[st] = sc_ref[st] + jnp.concatenate(cols, axis=1)
                return carry

            lax.fori_loop(0, ns, add_bias, 0)

        def max_body(st, m):
            return jnp.maximum(m, jnp.max(sc_ref[st], axis=1, keepdims=True))

        m = lax.fori_loop(0, ns, max_body, jnp.max(sn, axis=1, keepdims=True))

        def sum_body(st, l):
            return l + jnp.sum(jnp.exp2(sc_ref[st] - m), axis=1, keepdims=True)

        pn = jnp.exp2(sn - m)
        l = lax.fori_loop(0, ns, sum_body, jnp.sum(pn, axis=1, keepdims=True))
        il = 1.0 / l
        m_ref[...] = m
        il_ref[...] = il
        pn = (pn * il).astype(BF16)
        if moba:
            onew_ref[...] = jnp.dot(pn, vn, preferred_element_type=F32)
            acc_ref[...] = jnp.zeros_like(acc_ref)
        else:
            for h in range(B_HEADS):
                onew_ref[h] = jnp.dot(pn[2 * h * t_new:(2 * h + 2) * t_new],
                                      vn[:, h * LANES:(h + 1) * LANES], preferred_element_type=F32)
            acc_ref[...] = jnp.zeros_like(acc_ref)

    @pl.when(s >= ns)
    def _value_phase():
        p = (jnp.exp2(sc_ref[s - ns] - m_ref[...]) * il_ref[...]).astype(BF16)
        if moba:
            for k in range(pg):
                vstage_ref[:, k * PAGE_SIZE:(k + 1) * PAGE_SIZE] = vpages[k][...].astype(BF16)
            acc_ref[...] += lax.dot_general(p, vstage_ref[...], (((1,), (1,)), ((), ())),
                                            preferred_element_type=F32)
        else:
            for k in range(pg):
                for h in range(B_HEADS):
                    vh = vpages[k][pl.ds(h, PAGE_SIZE, stride=B_HEADS), :]
                    vstage_ref[h, k * PAGE_SIZE:(k + 1) * PAGE_SIZE, :] = vh.astype(BF16)
            for h in range(B_HEADS):
                acc_ref[h] += jnp.dot(p[2 * h * t_new:(2 * h + 2) * t_new], vstage_ref[h],
                                      preferred_element_type=F32)

    @pl.when(s == 2 * ns - 1)
    def _finalize():
        if moba:
            tot = acc_ref[...] + onew_ref[...]
            lane_o = lax.broadcasted_iota(jnp.int32, (t_new, WIDTH), 1)
            out = jnp.zeros((t_new, WIDTH), F32)
            for gidx in range(N_GROUPS):
                out = jnp.where(lane_o // HEAD_DIM == gidx, tot[gidx * t_new:(gidx + 1) * t_new], out)
            o_ref[...] = out
        else:
            lam = (jnp.exp(jnp.sum(lq1_ref[...] * lk1_ref[...], axis=1, keepdims=True))
                   - jnp.exp(jnp.sum(lq2_ref[...] * lk2_ref[...], axis=1, keepdims=True))
                   + LAM_INIT)
            for h in range(B_HEADS):
                tot = acc_ref[h] + onew_ref[h]
                d = tot[0:t_new] - lam * tot[t_new:2 * t_new]
                y = d * lax.rsqrt(jnp.mean(d * d, axis=-1, keepdims=True) + SUBLN_EPS) * sg_ref[...]
                o_ref[:, h * LANES:(h + 1) * LANES] = y * (1.0 - LAM_INIT)


def _decode_attn(page_table, qbd, k_new, v_new, cache_k, cache_v, extras, moba):
    db, n_pages = page_table.shape
    t_new = k_new.shape[1]
    pg = PAGES_PER_STEP
    ns = n_pages // pg

    ncol = N_GROUPS * t_new
    step_keys = pg * PAGE_SIZE
    per_seq = lambda shape: pl.BlockSpec((None,) + shape, lambda b, s, pt: (b, 0, 0))
    const = lambda shape: pl.BlockSpec(shape, lambda b, s, pt: (0, 0))
    in_specs = [per_seq((ncol, WIDTH)), per_seq((t_new, WIDTH)), per_seq((t_new, WIDTH))]
    if not moba:
        in_specs += [const((1, HEAD_DIM))] * 4 + [const((1, LANES))]
    in_specs += [pl.BlockSpec(memory_space=pl.ANY)] * 2
    scratch = [pltpu.VMEM((PAGE_SLOTS, pg, WIDTH, LANES), F32),
               pltpu.SemaphoreType.DMA((PAGE_SLOTS,)),
               pltpu.VMEM((ns, ncol, step_keys), F32),
               pltpu.VMEM((WIDTH, step_keys), BF16)]
    if moba:
        scratch += [pltpu.VMEM((WIDTH, step_keys), BF16),
                    pltpu.VMEM((ncol, WIDTH), F32), pltpu.VMEM((ncol, WIDTH), F32)]
    else:
        scratch += [pltpu.VMEM((B_HEADS, step_keys, LANES), BF16),
                    pltpu.VMEM((B_HEADS, 2 * t_new, LANES), F32),
                    pltpu.VMEM((B_HEADS, 2 * t_new, LANES), F32)]
    scratch += [pltpu.VMEM((ncol, 1), F32), pltpu.VMEM((ncol, 1), F32)]
    if moba:
        scratch += [pltpu.VMEM((WIDTH, LANES), F32)]
    grid_spec = pltpu.PrefetchScalarGridSpec(
        num_scalar_prefetch=1, grid=(db, 2 * ns), in_specs=in_specs,
        out_specs=pl.BlockSpec((None, t_new, WIDTH), lambda b, s, pt: (b, 0, 0)),
        scratch_shapes=scratch)
    return pl.pallas_call(
        functools.partial(_decode_kernel, moba=moba, n_pages=n_pages, t_new=t_new),
        grid_spec=grid_spec,
        out_shape=jax.ShapeDtypeStruct((db, t_new, WIDTH), F32),
        compiler_params=pltpu.CompilerParams(dimension_semantics=("arbitrary", "arbitrary"),
                                             vmem_limit_bytes=VMEM_LIMIT),
        name="moba_decode" if moba else "diff_decode",
    )(page_table, qbd, k_new, v_new, *extras, cache_k, cache_v)


def _block_diag_queries(q, db, t_new):
    q4 = q.reshape(db, t_new, N_GROUPS, HEAD_DIM)
    eye = jnp.eye(N_GROUPS, dtype=q.dtype)
    return jnp.einsum('btgd,hg->bhtgd', q4, eye).reshape(db, N_GROUPS * t_new, WIDTH)


def _pages_feature_major(cache):
    nd = cache.ndim
    moved = jnp.transpose(cache, (0,) + tuple(range(2, nd)) + (1,))
    return moved.reshape(cache.shape[0], WIDTH, cache.shape[1])


def _merge_mlp_kernel(x_ref, oa_ref, ob_ref, gate_ref, wpa_ref, wpb_ref, wo_ref, gm_ref,
                      wup_ref, wdn_ref, gf_ref, y_ref):
    ga = gate_ref[:, 0:D_MODEL]
    gb = gate_ref[:, D_MODEL:2 * D_MODEL]
    t = (ga * jnp.dot(oa_ref[...], wpa_ref[...], preferred_element_type=F32)
         + gb * jnp.dot(ob_ref[...], wpb_ref[...], preferred_element_type=F32))
    h = x_ref[...] + jnp.dot(t.astype(BF16), wo_ref[...], preferred_element_type=F32)
    hn = (h * lax.rsqrt(jnp.mean(h * h, axis=-1, keepdims=True) + NORM_EPS) * gm_ref[...]).astype(BF16)
    y = h
    chunk = D_FF // 4
    for c in range(D_FF // chunk):
        u = jnp.dot(hn, wup_ref[:, c * chunk:(c + 1) * chunk], preferred_element_type=F32)
        a = jnp.square(jnp.maximum(u, 0.0)).astype(BF16)
        y = y + jnp.dot(a, wdn_ref[c * chunk:(c + 1) * chunk, :], preferred_element_type=F32)
    y_ref[...] = y * lax.rsqrt(jnp.mean(y * y, axis=-1, keepdims=True) + NORM_EPS) * gf_ref[...]


def _merge_mlp(x, oa, ob, gates, w_pa, w_pb, w_o, norm_mlp, w_up, w_down, norm_final):
    t = x.shape[0]
    row = lambda w: pl.BlockSpec((TOK_TILE, w), lambda i: (i, 0))
    return pl.pallas_call(
        _merge_mlp_kernel,
        grid=(t // TOK_TILE,),
        in_specs=[row(D_MODEL), row(WIDTH), row(WIDTH), row(2 * D_MODEL),
                  _const_spec((WIDTH, D_MODEL)), _const_spec((WIDTH, D_MODEL)),
                  _const_spec((D_MODEL, D_MODEL)), _const_spec((1, D_MODEL)),
                  _const_spec((D_MODEL, D_FF)), _const_spec((D_FF, D_MODEL)),
                  _const_spec((1, D_MODEL))],
        out_specs=row(D_MODEL),
        out_shape=jax.ShapeDtypeStruct((t, D_MODEL), F32),
        compiler_params=pltpu.CompilerParams(dimension_semantics=("parallel",),
                                             vmem_limit_bytes=VMEM_LIMIT),
        name="merge_mlp",
    )(x, oa, ob, gates, w_pa, w_pb, w_o, norm_mlp, w_up, w_down, norm_final)


def kernel(x_prompt, x_sample, cache_moba_k, cache_moba_v, cache_diff_k, cache_diff_v, page_table,
           norm_attn, w_in, b_gate, lambda_q1, lambda_k1, lambda_q2, lambda_k2, subln_g,
           w_pa, w_pb, w_o, norm_mlp, w_up, w_down, norm_final):
    b, s, _ = x_prompt.shape
    db, t_new, _ = x_sample.shape
    n_pool = cache_moba_k.shape[0]
    past_len = page_table.shape[1] * PAGE_SIZE
    assert norm_attn.shape[0] == 1 and s % MOBA_BLOCK == 0 and s // MOBA_BLOCK <= 32
    assert (db * t_new) % TOK_TILE == 0 and past_len % MOBA_BLOCK == 0 and t_new == 8
    assert page_table.shape[1] % PAGES_PER_STEP == 0

    w_in_bf = w_in[0, :, :_OFF_END].astype(BF16)
    ng = norm_attn[0][None]
    bg = b_gate[0][None]
    lam_vecs = [v[0][None] for v in (lambda_q1, lambda_k1, lambda_q2, lambda_k2)]
    sg = subln_g[0][None]
    merge_w = (w_pa[0].astype(BF16), w_pb[0].astype(BF16), w_o[0].astype(BF16), norm_mlp[0][None],
               w_up[0].astype(BF16), w_down[0].astype(BF16), norm_final[None])

    xp = x_prompt.reshape(b * s, D_MODEL)
    tabs_p = _rope_tables(jnp.arange(s, dtype=jnp.int32))
    (ka, va, kb, vb, qa_bf, kaug, va_bf, qb_bf, kb_bf, vb_bf, gates_p, kmean) = _inproj(
        xp, ng, w_in_bf, bg, tabs_p, s // MOBA_BLOCK, device_layout=True)
    seq_minor = lambda a, dims: jnp.moveaxis(a.reshape((b, 1) + dims + (s,)), -1, 1)
    mhi, mlo = _moba_mean_mats(kmean, b, s)
    vt_view = lambda vt: vt.reshape(b, s // ATT_TILE, WIDTH // LANES, LANES, ATT_TILE)
    oa_p = _pair_attn(qa_bf.reshape(b, s, WIDTH), kaug.reshape(b, s, 2 * WIDTH),
                      vt_view(va_bf), (mhi, mlo), moba=True)
    ob_p = _pair_attn(qb_bf.reshape(b, s, WIDTH), kb_bf.reshape(b, s, WIDTH),
                      vt_view(vb_bf), (*lam_vecs, sg), moba=False)
    y_p = _merge_mlp(xp, oa_p.reshape(b * s, WIDTH), ob_p.reshape(b * s, WIDTH), gates_p, *merge_w)

    xs = x_sample.reshape(db * t_new, D_MODEL)
    tabs_s = tuple(jnp.tile(a, (db, 1)) for a in
                   _rope_tables(past_len + jnp.arange(t_new, dtype=jnp.int32)))
    (kas, vas, kbs, vbs, qas_bf, _, _, qbs_bf, _, _, gates_s, _) = _inproj(
        xs, ng, w_in_bf, bg, tabs_s, 1, device_layout=False)
    oa_s = _decode_attn(page_table, _block_diag_queries(qas_bf, db, t_new),
                        kas.reshape(db, t_new, WIDTH), vas.reshape(db, t_new, WIDTH),
                        _pages_feature_major(cache_moba_k), _pages_feature_major(cache_moba_v),
                        (), moba=True)
    ob_s = _decode_attn(page_table, _block_diag_queries(qbs_bf, db, t_new),
                        kbs.reshape(db, t_new, WIDTH), vbs.reshape(db, t_new, WIDTH),
                        _pages_feature_major(cache_diff_k),
                        cache_diff_v.reshape(n_pool, PAGE_SIZE * B_HEADS, LANES),
                        (*lam_vecs, sg), moba=False)
    y_s = _merge_mlp(xs, oa_s.reshape(db * t_new, WIDTH).astype(BF16),
                     ob_s.reshape(db * t_new, WIDTH).astype(BF16), gates_s, *merge_w)

    return (y_p.reshape(b, s, D_MODEL), y_s.reshape(db, t_new, D_MODEL),
            seq_minor(ka, (A_HEADS, HEAD_DIM)), seq_minor(va, (A_HEADS, HEAD_DIM)),
            seq_minor(kb, (B_HEADS, 2, HEAD_DIM)), vb.reshape(b, s, 1, B_HEADS, 2 * HEAD_DIM),
            kas.reshape(db, t_new, 1, A_HEADS, HEAD_DIM), vas.reshape(db, t_new, 1, A_HEADS, HEAD_DIM),
            kbs.reshape(db, t_new, 1, B_HEADS, 2, HEAD_DIM),
            vbs.reshape(db, t_new, 1, B_HEADS, 2 * HEAD_DIM))
```

```python
import functools
import math

import jax
import jax.numpy as jnp
from jax import lax
from jax.experimental import pallas as pl
from jax.experimental.pallas import tpu as pltpu

F32 = jnp.float32
BF16 = jnp.bfloat16

D_MODEL = 1024
HEAD_DIM = 64
A_HEADS = 8
B_HEADS = 4
WIDTH = 512
MOBA_BLOCK = 256
MOBA_TOPK = 3
ROT_DIM = HEAD_DIM // 4
ROPE_THETA = 500000.0
D_FF = 4 * D_MODEL
NORM_EPS = 1e-6
SUBLN_EPS = 1e-5
NEG = -1e30
PAGE_SIZE = 128
LAM_INIT = 0.8 - 0.6 * math.exp(-0.3 * 0)
SCALE = HEAD_DIM ** -0.5
QSCALE = SCALE * math.log2(math.e)

LANES = 128
TOK_TILE = 256
ATT_TILE = 256
ATT_PAIRS = 4
VT_ROWS = LANES + 16
PAGES_PER_STEP = 16
PAGE_SLOTS = 4
N_GROUPS = 8
VMEM_LIMIT = 56 * 1024 * 1024

_OFF_QA, _OFF_KA, _OFF_VA, _OFF_QB, _OFF_KB, _OFF_VB, _OFF_G, _OFF_END = (
    0, 512, 1024, 1536, 2048, 2560, 3072, 5120)


def _const_spec(shape):
    nd = len(shape)
    return pl.BlockSpec(shape, lambda *_: (0,) * nd, pipeline_mode=pl.Buffered(1))


def _rope_tables(pos):
    half = ROT_DIM // 2
    inv = ROPE_THETA ** (-jnp.arange(half, dtype=F32) * 2.0 / ROT_DIM)
    ang = pos.astype(F32)[:, None] * inv[None, :]
    cos, sin = jnp.cos(ang), jnp.sin(ang)
    t = pos.shape[0]
    ones = jnp.ones((t, HEAD_DIM - ROT_DIM), F32)
    zeros = jnp.zeros((t, HEAD_DIM - ROT_DIM), F32)
    zh = jnp.zeros((t, half), F32)
    c = jnp.concatenate([cos, cos, ones], axis=1)
    s1 = jnp.concatenate([-sin, zh, zeros], axis=1)
    s2 = jnp.concatenate([zh, sin, zeros], axis=1)
    return tuple(jnp.concatenate([a, a], axis=1) for a in (c, s1, s2))


def _inproj_kernel(x_ref, g_ref, w_ref, bg_ref, c_ref, s1_ref, s2_ref,
                   ka_ref, va_ref, kb_ref, vb_ref,
                   qa_ref, kaug_ref, vab_ref, qb_ref, kbb_ref, vbb_ref,
                   gate_ref, kmean_ref, *, blocks_per_seq, device_layout):
    x = x_ref[...]
    ms = jnp.mean(x * x, axis=-1, keepdims=True)
    xn = (x * lax.rsqrt(ms + NORM_EPS) * g_ref[...]).astype(BF16)
    c, s1, s2 = c_ref[...], s1_ref[...], s2_ref[...]

    def proj(lo, hi):
        return jnp.dot(xn, w_ref[:, lo:hi], preferred_element_type=F32)

    def rope(z):
        outs = []
        for k in range(z.shape[1] // LANES):
            zc = z[:, k * LANES:(k + 1) * LANES]
            outs.append(zc * c + pltpu.roll(zc, LANES - ROT_DIM // 2, 1) * s1
                        + pltpu.roll(zc, ROT_DIM // 2, 1) * s2)
        return jnp.concatenate(outs, axis=1)

    qa = rope(proj(_OFF_QA, _OFF_KA))
    qa_ref[...] = (qa * QSCALE).astype(BF16)

    def store_kv(ref, val, val_t=None):
        if device_layout:
            ref[...] = val.T if val_t is None else val_t
        else:
            ref[...] = val

    def store_vt(ref, vt):
        for p in range(WIDTH // LANES):
            ref[0, p, 0:LANES, :] = vt[p * LANES:(p + 1) * LANES].astype(BF16)
            ref[0, p, LANES:VT_ROWS, :] = jnp.ones((VT_ROWS - LANES, vt.shape[1]), BF16)

    ka = rope(proj(_OFF_KA, _OFF_VA))
    store_kv(ka_ref, ka)
    kmean_ref[0] = jnp.mean(ka, axis=0, keepdims=True)
    blk = pl.program_id(0) % blocks_per_seq
    lane = lax.broadcasted_iota(jnp.int32, (x.shape[0], LANES), 1)
    onehot = jnp.where((lane == blk) | (lane == blk + 32), 1.0, 0.0).astype(BF16)
    ka_bf = ka.astype(BF16)
    for p in range(WIDTH // LANES):
        kaug_ref[:, 2 * p * LANES:(2 * p + 1) * LANES] = ka_bf[:, p * LANES:(p + 1) * LANES]
        kaug_ref[:, (2 * p + 1) * LANES:(2 * p + 2) * LANES] = onehot

    va = proj(_OFF_VA, _OFF_QB)
    va_t = va.T
    store_kv(va_ref, va, va_t)
    store_vt(vab_ref, va_t)

    qb = rope(proj(_OFF_QB, _OFF_KB))
    qb_ref[...] = (qb * QSCALE).astype(BF16)

    kb = rope(proj(_OFF_KB, _OFF_VB))
    store_kv(kb_ref, kb)
    kbb_ref[...] = kb.astype(BF16)

    vb = proj(_OFF_VB, _OFF_G)
    if device_layout:
        for h in range(B_HEADS):
            vb_ref[pl.ds(h, x.shape[0], stride=B_HEADS), :] = vb[:, h * LANES:(h + 1) * LANES]
    else:
        vb_ref[...] = vb
    store_vt(vbb_ref, vb.T)

    z = proj(_OFF_G, _OFF_END) + bg_ref[...]
    gate_ref[...] = 1.0 / (1.0 + jnp.exp(-z))


def _inproj(x, norm_g, w_in, b_gate, tables, blocks_per_seq, device_layout):
    t = x.shape[0]
    nt = t // TOK_TILE
    tab_tiles = tables[0].shape[0] // TOK_TILE
    row = lambda w: pl.BlockSpec((TOK_TILE, w), lambda i: (i, 0))
    tab = pl.BlockSpec((TOK_TILE, LANES), lambda i: (i % tab_tiles, 0))
    bf16o = jax.ShapeDtypeStruct((t, WIDTH), BF16)
    vt_spec = pl.BlockSpec((1, WIDTH // LANES, VT_ROWS, TOK_TILE), lambda i: (i, 0, 0, 0))
    vt_shape = jax.ShapeDtypeStruct((nt, WIDTH // LANES, VT_ROWS, TOK_TILE), BF16)
    if device_layout:
        nb = nt // blocks_per_seq
        kv_spec = pl.BlockSpec((None, WIDTH, TOK_TILE),
                               lambda i: (i // blocks_per_seq, 0, i % blocks_per_seq))
        kv_shape = jax.ShapeDtypeStruct((nb, WIDTH, blocks_per_seq * TOK_TILE), F32)
        f32_specs = [kv_spec] * 3 + [pl.BlockSpec((TOK_TILE * B_HEADS, LANES), lambda i: (i, 0))]
        f32_shapes = [kv_shape] * 3 + [jax.ShapeDtypeStruct((t * B_HEADS, LANES), F32)]
    else:
        f32_specs = [row(WIDTH)] * 4
        f32_shapes = [jax.ShapeDtypeStruct((t, WIDTH), F32)] * 4
    return pl.pallas_call(
        functools.partial(_inproj_kernel, blocks_per_seq=blocks_per_seq,
                          device_layout=device_layout),
        grid=(nt,),
        in_specs=[row(D_MODEL), _const_spec((1, D_MODEL)), _const_spec((D_MODEL, _OFF_END)),
                  _const_spec((1, 2 * D_MODEL)), tab, tab, tab],
        out_specs=f32_specs + [row(WIDTH), row(2 * WIDTH), vt_spec, row(WIDTH), row(WIDTH), vt_spec]
                  + [row(2 * D_MODEL), pl.BlockSpec((1, 1, WIDTH), lambda i: (i, 0, 0))],
        out_shape=f32_shapes + [bf16o, jax.ShapeDtypeStruct((t, 2 * WIDTH), BF16), vt_shape, bf16o,
                                bf16o, vt_shape]
                  + [jax.ShapeDtypeStruct((t, 2 * D_MODEL), F32),
                     jax.ShapeDtypeStruct((nt, 1, WIDTH), F32)],
        compiler_params=pltpu.CompilerParams(dimension_semantics=("parallel",),
                                             vmem_limit_bytes=VMEM_LIMIT),
        name="inproj",
    )(x, norm_g, w_in, b_gate, *tables)


def _top3_bias(g, valid, forced, seg_masks, idx, axis):
    selected = forced
    big = jnp.float32(1 << 20)
    for seg in seg_masks:
        gh = jnp.where(seg, jnp.where(valid, g, NEG), -jnp.inf)
        for _ in range(MOBA_TOPK):
            mx = jnp.max(gh, axis=axis, keepdims=True)
            first = jnp.min(jnp.where(gh == mx, idx, big), axis=axis, keepdims=True)
            pick = idx == first
            selected = selected | (pick & valid)
            gh = jnp.where(pick, -jnp.inf, gh)
    return jnp.where(selected, 0.0, NEG)


def _pair_attn_kernel(*refs, moba, pairs):
    qs_ref, s_ref, p_ref, acc_ref, m_ref, c_ref, a_ref = refs[-7:]
    if moba:
        q_ref, k_ref, v_ref, mhi_ref, mlo_ref, o_ref = refs[:-7]
    else:
        q_ref, k_ref, v_ref, lq1_ref, lk1_ref, lq2_ref, lk2_ref, sg_ref, o_ref = refs[:-7]
    i = pl.program_id(2)
    tq = ATT_TILE
    kw = k_ref.shape[1] // pairs
    chains = range(2 * pairs)
    lane = lax.broadcasted_iota(jnp.int32, (tq, LANES), 1)
    for pr in range(pairs):
        q = q_ref[:, pr * LANES:(pr + 1) * LANES]
        zero = jnp.zeros_like(q)
        qs_ref[2 * pr, :, 0:LANES] = jnp.where(lane < HEAD_DIM, q, zero)
        qs_ref[2 * pr + 1, :, 0:LANES] = jnp.where(lane >= HEAD_DIM, q, zero)
        if moba:
            contract_last = (((1,), (1,)), ((), ()))
            g = (lax.dot_general(mhi_ref[pr], q, contract_last, preferred_element_type=F32)
                 + lax.dot_general(mlo_ref[pr], q, contract_last, preferred_element_type=F32))
            ridx = lax.broadcasted_iota(jnp.int32, g.shape, 0)
            blk = ridx & 31
            bias_t = _top3_bias(g, blk < i, blk == i, [ridx < 32, ridx >= 32], ridx.astype(F32), 0)
            bias = jnp.concatenate([bias_t, jnp.zeros_like(bias_t)], axis=0).T.astype(BF16)
            zb = jnp.zeros_like(bias)
            qs_ref[2 * pr, :, LANES:2 * LANES] = jnp.where(lane < 32, bias, zb)
            qs_ref[2 * pr + 1, :, LANES:2 * LANES] = jnp.where((lane >= 32) & (lane < 64), bias, zb)

    def scores(j, ch):
        start = pl.multiple_of(j * ATT_TILE, ATT_TILE)
        kb = k_ref[pl.ds(start, ATT_TILE), (ch // 2) * kw:(ch // 2 + 1) * kw]
        st = lax.dot_general(kb, qs_ref[ch], (((1,), (1,)), ((), ())),
                             preferred_element_type=F32)
        s_ref[ch] = st
        c_ref[ch] = jnp.max(st, axis=0, keepdims=True)

    def softmax(ch, diagonal):
        st = s_ref[ch]
        if diagonal:
            kidx = lax.broadcasted_iota(jnp.int32, (ATT_TILE, tq), 0)
            qidx = lax.broadcasted_iota(jnp.int32, (ATT_TILE, tq), 1)
            st = jnp.where(kidx <= qidx, st, NEG)
            cmax = jnp.max(st, axis=0, keepdims=True)
        else:
            cmax = c_ref[ch]
        m = m_ref[ch]
        m_new = jnp.maximum(m, cmax)
        p_ref[ch] = jnp.exp2(st - m_new).astype(BF16)
        m_ref[ch] = m_new
        a_ref[ch] = jnp.exp2(m - m_new)

    def value(jv, ch):
        acc_ref[ch] = a_ref[ch] * acc_ref[ch] + jnp.dot(v_ref[jv, ch // 2], p_ref[ch],
                                                        preferred_element_type=F32)

    def body(j, carry):
        for ch in chains:
            value(jnp.maximum(j - 1, 0), ch)
            softmax(ch, False)
            scores(j + 1, ch)
        return carry

    for ch in chains:
        scores(0, ch)
        p_ref[ch] = jnp.zeros((ATT_TILE, tq), BF16)
        acc_ref[ch] = jnp.zeros((VT_ROWS, tq), F32)
        m_ref[ch] = jnp.full((1, tq), -jnp.inf, F32)
        a_ref[ch] = jnp.ones((1, tq), F32)
    lax.fori_loop(0, i, body, 0)
    if not moba:
        lam = (jnp.exp(jnp.sum(lq1_ref[...] * lk1_ref[...], axis=1, keepdims=True))
               - jnp.exp(jnp.sum(lq2_ref[...] * lk2_ref[...], axis=1, keepdims=True)) + LAM_INIT)
    for pr in range(pairs):
        ots = []
        for ch in (2 * pr, 2 * pr + 1):
            value(jnp.maximum(i - 1, 0), ch)
            softmax(ch, True)
            value(i, ch)
            acc = acc_ref[ch]
            ots.append(acc[0:LANES] * (1.0 / acc[LANES:LANES + 1]))
        ot0, ot1 = ots
        if moba:
            ot = jnp.concatenate([ot0[0:HEAD_DIM], ot1[HEAD_DIM:LANES]], axis=0)
            o_ref[:, pr * LANES:(pr + 1) * LANES] = ot.T.astype(o_ref.dtype)
        else:
            d = (ot0 - lam * ot1).T
            y = d * lax.rsqrt(jnp.mean(d * d, axis=-1, keepdims=True) + SUBLN_EPS) * sg_ref[...]
            o_ref[:, pr * LANES:(pr + 1) * LANES] = (y * (1.0 - LAM_INIT)).astype(o_ref.dtype)


def _pair_attn(q, k, v, extras, moba):
    b, s, _ = q.shape
    npair = WIDTH // LANES
    kw = k.shape[2] // npair
    nblk = s // ATT_TILE
    pairs = ATT_PAIRS
    nch = 2 * pairs
    in_specs = [pl.BlockSpec((None, ATT_TILE, pairs * LANES), lambda bi, g, i: (bi, i, g)),
                pl.BlockSpec((None, s, pairs * kw), lambda bi, g, i: (bi, 0, g),
                             pipeline_mode=pl.Buffered(1)),
                pl.BlockSpec((None, nblk, pairs, VT_ROWS, ATT_TILE), lambda bi, g, i: (bi, 0, g, 0, 0),
                             pipeline_mode=pl.Buffered(1))]
    if moba:
        in_specs += [pl.BlockSpec((None, pairs, 2 * 32, LANES), lambda bi, g, i: (bi, g, 0, 0))] * 2
    else:
        in_specs += [_const_spec((1, HEAD_DIM))] * 4 + [_const_spec((1, LANES))]
    stat = pltpu.VMEM((nch, 1, ATT_TILE), F32)
    return pl.pallas_call(
        functools.partial(_pair_attn_kernel, moba=moba, pairs=pairs),
        grid=(b, npair // pairs, s // ATT_TILE),
        in_specs=in_specs,
        out_specs=pl.BlockSpec((None, ATT_TILE, pairs * LANES), lambda bi, g, i: (bi, i, g)),
        out_shape=jax.ShapeDtypeStruct((b, s, WIDTH), BF16),
        scratch_shapes=[pltpu.VMEM((nch, ATT_TILE, kw), BF16),
                        pltpu.VMEM((nch, ATT_TILE, ATT_TILE), F32),
                        pltpu.VMEM((nch, ATT_TILE, ATT_TILE), BF16),
                        pltpu.VMEM((nch, VT_ROWS, ATT_TILE), F32),
                        stat, stat, stat],
        compiler_params=pltpu.CompilerParams(
            dimension_semantics=("parallel", "parallel", "arbitrary"),
            vmem_limit_bytes=VMEM_LIMIT),
        name="moba_prompt" if moba else "diff_prompt",
    )(q, k, v, *extras)


def _moba_mean_mats(kmean, b, s):
    nblk = s // MOBA_BLOCK
    mean = kmean.reshape(b, nblk, WIDTH // LANES, 2, HEAD_DIM)
    mt = jnp.transpose(mean, (0, 2, 3, 1, 4))
    z = jnp.zeros((b, WIDTH // LANES, 2, 32, HEAD_DIM), F32).at[:, :, :, :nblk].set(mt)
    eye = jnp.eye(2, dtype=F32)
    m = jnp.einsum('bphnd,hg->bphngd', z, eye).reshape(b, WIDTH // LANES, 2 * 32, LANES)
    hi = m.astype(BF16)
    lo = (m - hi.astype(F32)).astype(BF16)
    return hi, lo


def _decode_kernel(*refs, moba, n_pages, t_new):
    pg = PAGES_PER_STEP
    ns = n_pages // pg
    step_keys = pg * PAGE_SIZE
    ppb = MOBA_BLOCK // PAGE_SIZE
    bps = pg // ppb
    ncol = N_GROUPS * t_new
    pt_ref, qbd_ref, knew_ref, vnew_ref = refs[:4]
    pos = 4
    if not moba:
        lq1_ref, lk1_ref, lq2_ref, lk2_ref, sg_ref = refs[pos:pos + 5]
        pos += 5
    ck_ref, cv_ref, o_ref, ring_ref, sem_ref = refs[pos:pos + 5]
    sc_ref, kstage_ref, vstage_ref, acc_ref, onew_ref, m_ref, il_ref = refs[pos + 5:pos + 12]
    if moba:
        mean_ref, = refs[pos + 12:pos + 13]
    b = pl.program_id(0)
    s = pl.program_id(1)
    lane = lax.broadcasted_iota(jnp.int32, (ncol, LANES), 1)
    row = lax.broadcasted_iota(jnp.int32, (ncol, LANES), 0)

    steps = 2 * ns
    g = b * steps + s
    total = pl.num_programs(0) * steps

    def page_copy(cache_ref, page, slot, k):
        return pltpu.make_async_copy(cache_ref.at[page], ring_ref.at[slot, k], sem_ref.at[slot])

    def start_batch(gb):
        bb = gb // steps
        sb = gb - bb * steps
        slot = gb % PAGE_SLOTS

        @pl.when(sb < ns)
        def _():
            for k in range(pg):
                page_copy(ck_ref, pt_ref[bb, sb * pg + k], slot, k).start()

        @pl.when(sb >= ns)
        def _():
            for k in range(pg):
                page_copy(cv_ref, pt_ref[bb, (sb - ns) * pg + k], slot, k).start()

    @pl.when(g == 0)
    def _prime():
        for ahead in range(PAGE_SLOTS - 1):
            start_batch(g + ahead)

    @pl.when(g + PAGE_SLOTS - 1 < total)
    def _prefetch():
        start_batch(g + PAGE_SLOTS - 1)

    slot = g % PAGE_SLOTS
    for k in range(pg):
        page_copy(ck_ref, 0, slot, k).wait()
    kpages = [ring_ref.at[slot, k] for k in range(pg)]
    vpages = kpages

    if moba:
        @pl.when(s == 0)
        def _init():
            mean_ref[...] = jnp.zeros_like(mean_ref)

    @pl.when(s < ns)
    def _key_phase():
        for k in range(pg):
            kp = kpages[k][...]
            kstage_ref[:, k * PAGE_SIZE:(k + 1) * PAGE_SIZE] = kp.astype(BF16)
            if moba and k % ppb == ppb - 1:
                tot = kp
                for kk in range(1, ppb):
                    tot = tot + kpages[k - kk][...]
                mean = jnp.sum(tot, axis=1, keepdims=True) * (1.0 / MOBA_BLOCK)
                n = s * bps + k // ppb
                lane_m = lax.broadcasted_iota(jnp.int32, (WIDTH, LANES), 1)
                mean_ref[...] = jnp.where(lane_m == n, mean, mean_ref[...])
        sc_ref[s] = jnp.dot(qbd_ref[...], kstage_ref[...], preferred_element_type=F32)

    @pl.when(s == ns)
    def _softmax_stats():
        pad = PAGE_SIZE - t_new
        kn = jnp.concatenate([knew_ref[...], jnp.zeros((pad, WIDTH), F32)], axis=0).astype(BF16)
        vn = jnp.concatenate([vnew_ref[...], jnp.zeros((pad, WIDTH), F32)], axis=0).astype(BF16)
        sn = lax.dot_general(qbd_ref[...], kn, (((1,), (1,)), ((), ())),
                             preferred_element_type=F32)
        sn = jnp.where((lane < t_new) & (lane <= (row & (t_new - 1))), sn, NEG)
        if moba:
            mean = mean_ref[...]
            mhi = mean.astype(BF16)
            mlo = (mean - mhi.astype(F32)).astype(BF16)
            g = (jnp.dot(qbd_ref[...], mhi, preferred_element_type=F32)
                 + jnp.dot(qbd_ref[...], mlo, preferred_element_type=F32))
            valid = lane < n_pages // ppb
            bias = _top3_bias(g, valid, lane < 0, [lane >= 0], lane.astype(F32), 1)

            def add_bias(st, carry):
                cols = []
                for bi in range(bps):
                    col = jnp.sum(jnp.where(lane == st * bps + bi, bias, 0.0), axis=1, keepdims=True)
                    cols.append(jnp.broadcast_to(col, (ncol, MOBA_BLOCK)))
                sc_ref[st] = sc_ref[st] + jnp.concatenate(cols, axis=1)
                return carry

            lax.fori_loop(0, ns, add_bias, 0)

        def max_body(st, m):
            return jnp.maximum(m, jnp.max(sc_ref[st], axis=1, keepdims=True))

        m = lax.fori_loop(0, ns, max_body, jnp.max(sn, axis=1, keepdims=True))

        def sum_body(st, l):
            return l + jnp.sum(jnp.exp2(sc_ref[st] - m), axis=1, keepdims=True)

        pn = jnp.exp2(sn - m)
        l = lax.fori_loop(0, ns, sum_body, jnp.sum(pn, axis=1, keepdims=True))
        il = 1.0 / l
        m_ref[...] = m
        il_ref[...] = il
        pn = (pn * il).astype(BF16)
        if moba:
            onew_ref[...] = jnp.dot(pn, vn, preferred_element_type=F32)
            acc_ref[...] = jnp.zeros_like(acc_ref)
        else:
            for h in range(B_HEADS):
                onew_ref[h] = jnp.dot(pn[2 * h * t_new:(2 * h + 2) * t_new],
                                      vn[:, h * LANES:(h + 1) * LANES], preferred_element_type=F32)
            acc_ref[...] = jnp.zeros_like(acc_ref)

    @pl.when(s >= ns)
    def _value_phase():
        p = (jnp.exp2(sc_ref[s - ns] - m_ref[...]) * il_ref[...]).astype(BF16)
        if moba:
            for k in range(pg):
                vstage_ref[:, k * PAGE_SIZE:(k + 1) * PAGE_SIZE] = vpages[k][...].astype(BF16)
            acc_ref[...] += lax.dot_general(p, vstage_ref[...], (((1,), (1,)), ((), ())),
                                            preferred_element_type=F32)
        else:
            for k in range(pg):
                for h in range(B_HEADS):
                    vh = vpages[k][pl.ds(h, PAGE_SIZE, stride=B_HEADS), :]
                    vstage_ref[h, k * PAGE_SIZE:(k + 1) * PAGE_SIZE, :] = vh.astype(BF16)
            for h in range(B_HEADS):
                acc_ref[h] += jnp.dot(p[2 * h * t_new:(2 * h + 2) * t_new], vstage_ref[h],
                                      preferred_element_type=F32)

    @pl.when(s == 2 * ns - 1)
    def _finalize():
        if moba:
            tot = acc_ref[...] + onew_ref[...]
            lane_o = lax.broadcasted_iota(jnp.int32, (t_new, WIDTH), 1)
            out = jnp.zeros((t_new, WIDTH), F32)
            for gidx in range(N_GROUPS):
                out = jnp.where(lane_o // HEAD_DIM == gidx, tot[gidx * t_new:(gidx + 1) * t_new], out)
            o_ref[...] = out
        else:
            lam = (jnp.exp(jnp.sum(lq1_ref[...] * lk1_ref[...], axis=1, keepdims=True))
                   - jnp.exp(jnp.sum(lq2_ref[...] * lk2_ref[...], axis=1, keepdims=True))
                   + LAM_INIT)
            for h in range(B_HEADS):
                tot = acc_ref[h] + onew_ref[h]
                d = tot[0:t_new] - lam * tot[t_new:2 * t_new]
                y = d * lax.rsqrt(jnp.mean(d * d, axis=-1, keepdims=True) + SUBLN_EPS) * sg_ref[...]
                o_ref[:, h * LANES:(h + 1) * LANES] = y * (1.0 - LAM_INIT)


def _decode_attn(page_table, qbd, k_new, v_new, cache_k, cache_v, extras, moba):
    db, n_pages = page_table.shape
    t_new = k_new.shape[1]
    pg = PAGES_PER_STEP
    ns = n_pages // pg

    ncol = N_GROUPS * t_new
    step_keys = pg * PAGE_SIZE
    per_seq = lambda shape: pl.BlockSpec((None,) + shape, lambda b, s, pt: (b, 0, 0))
    const = lambda shape: pl.BlockSpec(shape, lambda b, s, pt: (0, 0))
    in_specs = [per_seq((ncol, WIDTH)), per_seq((t_new, WIDTH)), per_seq((t_new, WIDTH))]
    if not moba:
        in_specs += [const((1, HEAD_DIM))] * 4 + [const((1, LANES))]
    in_specs += [pl.BlockSpec(memory_space=pl.ANY)] * 2
    scratch = [pltpu.VMEM((PAGE_SLOTS, pg, WIDTH, LANES), F32),
               pltpu.SemaphoreType.DMA((PAGE_SLOTS,)),
               pltpu.VMEM((ns, ncol, step_keys), F32),
               pltpu.VMEM((WIDTH, step_keys), BF16)]
    if moba:
        scratch += [pltpu.VMEM((WIDTH, step_keys), BF16),
                    pltpu.VMEM((ncol, WIDTH), F32), pltpu.VMEM((ncol, WIDTH), F32)]
    else:
        scratch += [pltpu.VMEM((B_HEADS, step_keys, LANES), BF16),
                    pltpu.VMEM((B_HEADS, 2 * t_new, LANES), F32),
                    pltpu.VMEM((B_HEADS, 2 * t_new, LANES), F32)]
    scratch += [pltpu.VMEM((ncol, 1), F32), pltpu.VMEM((ncol, 1), F32)]
    if moba:
        scratch += [pltpu.VMEM((WIDTH, LANES), F32)]
    grid_spec = pltpu.PrefetchScalarGridSpec(
        num_scalar_prefetch=1, grid=(db, 2 * ns), in_specs=in_specs,
        out_specs=pl.BlockSpec((None, t_new, WIDTH), lambda b, s, pt: (b, 0, 0)),
        scratch_shapes=scratch)
    return pl.pallas_call(
        functools.partial(_decode_kernel, moba=moba, n_pages=n_pages, t_new=t_new),
        grid_spec=grid_spec,
        out_shape=jax.ShapeDtypeStruct((db, t_new, WIDTH), F32),
        compiler_params=pltpu.CompilerParams(dimension_semantics=("arbitrary", "arbitrary"),
                                             vmem_limit_bytes=VMEM_LIMIT),
        name="moba_decode" if moba else "diff_decode",
    )(page_table, qbd, k_new, v_new, *extras, cache_k, cache_v)


def _block_diag_queries(q, db, t_new):
    q4 = q.reshape(db, t_new, N_GROUPS, HEAD_DIM)
    eye = jnp.eye(N_GROUPS, dtype=q.dtype)
    return jnp.einsum('btgd,hg->bhtgd', q4, eye).reshape(db, N_GROUPS * t_new, WIDTH)


def _pages_feature_major(cache):
    nd = cache.ndim
    moved = jnp.transpose(cache, (0,) + tuple(range(2, nd)) + (1,))
    return moved.reshape(cache.shape[0], WIDTH, cache.shape[1])


def _merge_mlp_kernel(x_ref, oa_ref, ob_ref, gate_ref, wpa_ref, wpb_ref, wo_ref, gm_ref,
                      wup_ref, wdn_ref, gf_ref, y_ref):
    ga = gate_ref[:, 0:D_MODEL]
    gb = gate_ref[:, D_MODEL:2 * D_MODEL]
    t = (ga * jnp.dot(oa_ref[...], wpa_ref[...], preferred_element_type=F32)
         + gb * jnp.dot(ob_ref[...], wpb_ref[...], preferred_element_type=F32))
    h = x_ref[...] + jnp.dot(t.astype(BF16), wo_ref[...], preferred_element_type=F32)
    hn = (h * lax.rsqrt(jnp.mean(h * h, axis=-1, keepdims=True) + NORM_EPS) * gm_ref[...]).astype(BF16)
    y = h
    chunk = D_FF // 4
    for c in range(D_FF // chunk):
        u = jnp.dot(hn, wup_ref[:, c * chunk:(c + 1) * chunk], preferred_element_type=F32)
        a = jnp.square(jnp.maximum(u, 0.0)).astype(BF16)
        y = y + jnp.dot(a, wdn_ref[c * chunk:(c + 1) * chunk, :], preferred_element_type=F32)
    y_ref[...] = y * lax.rsqrt(jnp.mean(y * y, axis=-1, keepdims=True) + NORM_EPS) * gf_ref[...]


def _merge_mlp(x, oa, ob, gates, w_pa, w_pb, w_o, norm_mlp, w_up, w_down, norm_final):
    t = x.shape[0]
    row = lambda w: pl.BlockSpec((TOK_TILE, w), lambda i: (i, 0))
    return pl.pallas_call(
        _merge_mlp_kernel,
        grid=(t // TOK_TILE,),
        in_specs=[row(D_MODEL), row(WIDTH), row(WIDTH), row(2 * D_MODEL),
                  _const_spec((WIDTH, D_MODEL)), _const_spec((WIDTH, D_MODEL)),
                  _const_spec((D_MODEL, D_MODEL)), _const_spec((1, D_MODEL)),
                  _const_spec((D_MODEL, D_FF)), _const_spec((D_FF, D_MODEL)),
                  _const_spec((1, D_MODEL))],
        out_specs=row(D_MODEL),
        out_shape=jax.ShapeDtypeStruct((t, D_MODEL), F32),
        compiler_params=pltpu.CompilerParams(dimension_semantics=("parallel",),
                                             vmem_limit_bytes=VMEM_LIMIT),
        name="merge_mlp",
    )(x, oa, ob, gates, w_pa, w_pb, w_o, norm_mlp, w_up, w_down, norm_final)


def kernel(x_prompt, x_sample, cache_moba_k, cache_moba_v, cache_diff_k, cache_diff_v, page_table,
           norm_attn, w_in, b_gate, lambda_q1, lambda_k1, lambda_q2, lambda_k2, subln_g,
           w_pa, w_pb, w_o, norm_mlp, w_up, w_down, norm_final):
    b, s, _ = x_prompt.shape
    db, t_new, _ = x_sample.shape
    n_pool = cache_moba_k.shape[0]
    past_len = page_table.shape[1] * PAGE_SIZE
    assert norm_attn.shape[0] == 1 and s % MOBA_BLOCK == 0 and s // MOBA_BLOCK <= 32
    assert (db * t_new) % TOK_TILE == 0 and past_len % MOBA_BLOCK == 0 and t_new == 8
    assert page_table.shape[1] % PAGES_PER_STEP == 0

    w_in_bf = w_in[0, :, :_OFF_END].astype(BF16)
    ng = norm_attn[0][None]
    bg = b_gate[0][None]
    lam_vecs = [v[0][None] for v in (lambda_q1, lambda_k1, lambda_q2, lambda_k2)]
    sg = subln_g[0][None]
    merge_w = (w_pa[0].astype(BF16), w_pb[0].astype(BF16), w_o[0].astype(BF16), norm_mlp[0][None],
               w_up[0].astype(BF16), w_down[0].astype(BF16), norm_final[None])

    xp = x_prompt.reshape(b * s, D_MODEL)
    tabs_p = _rope_tables(jnp.arange(s, dtype=jnp.int32))
    (ka, va, kb, vb, qa_bf, kaug, va_bf, qb_bf, kb_bf, vb_bf, gates_p, kmean) = _inproj(
        xp, ng, w_in_bf, bg, tabs_p, s // MOBA_BLOCK, device_layout=True)
    seq_minor = lambda a, dims: jnp.moveaxis(a.reshape((b, 1) + dims + (s,)), -1, 1)
    mhi, mlo = _moba_mean_mats(kmean, b, s)
    vt_view = lambda vt: vt.reshape(b, s // ATT_TILE, WIDTH // LANES, VT_ROWS, ATT_TILE)
    oa_p = _pair_attn(qa_bf.reshape(b, s, WIDTH), kaug.reshape(b, s, 2 * WIDTH),
                      vt_view(va_bf), (mhi, mlo), moba=True)
    ob_p = _pair_attn(qb_bf.reshape(b, s, WIDTH), kb_bf.reshape(b, s, WIDTH),
                      vt_view(vb_bf), (*lam_vecs, sg), moba=False)
    y_p = _merge_mlp(xp, oa_p.reshape(b * s, WIDTH), ob_p.reshape(b * s, WIDTH), gates_p, *merge_w)

    xs = x_sample.reshape(db * t_new, D_MODEL)
    tabs_s = tuple(jnp.tile(a, (db, 1)) for a in
                   _rope_tables(past_len + jnp.arange(t_new, dtype=jnp.int32)))
    (kas, vas, kbs, vbs, qas_bf, _, _, qbs_bf, _, _, gates_s, _) = _inproj(
        xs, ng, w_in_bf, bg, tabs_s, 1, device_layout=False)
    oa_s = _decode_attn(page_table, _block_diag_queries(qas_bf, db, t_new),
                        kas.reshape(db, t_new, WIDTH), vas.reshape(db, t_new, WIDTH),
                        _pages_feature_major(cache_moba_k), _pages_feature_major(cache_moba_v),
                        (), moba=True)
    ob_s = _decode_attn(page_table, _block_diag_queries(qbs_bf, db, t_new),
                        kbs.reshape(db, t_new, WIDTH), vbs.reshape(db, t_new, WIDTH),
                        _pages_feature_major(cache_diff_k),
                        cache_diff_v.reshape(n_pool, PAGE_SIZE * B_HEADS, LANES),
                        (*lam_vecs, sg), moba=False)
    y_s = _merge_mlp(xs, oa_s.reshape(db * t_new, WIDTH).astype(BF16),
                     ob_s.reshape(db * t_new, WIDTH).astype(BF16), gates_s, *merge_w)

    return (y_p.reshape(b, s, D_MODEL), y_s.reshape(db, t_new, D_MODEL),
            seq_minor(ka, (A_HEADS, HEAD_DIM)), seq_minor(va, (A_HEADS, HEAD_DIM)),
            seq_minor(kb, (B_HEADS, 2, HEAD_DIM)), vb.reshape(b, s, 1, B_HEADS, 2 * HEAD_DIM),
            kas.reshape(db, t_new, 1, A_HEADS, HEAD_DIM), vas.reshape(db, t_new, 1, A_HEADS, HEAD_DIM),
            kbs.reshape(db, t_new, 1, B_HEADS, 2, HEAD_DIM),
            vbs.reshape(db, t_new, 1, B_HEADS, 2 * HEAD_DIM))
```

```python
import functools
import math

import jax
import jax.numpy as jnp
import numpy as np
from jax import lax
from jax.experimental import pallas as pl
from jax.experimental.pallas import tpu as pltpu

F32 = jnp.float32
BF16 = jnp.bfloat16

D_MODEL = 1024
HEAD_DIM = 64
A_HEADS = 8
B_HEADS = 4
WIDTH = 512
MOBA_BLOCK = 256
MOBA_TOPK = 3
ROT_DIM = HEAD_DIM // 4
ROPE_THETA = 500000.0
D_FF = 4 * D_MODEL
NORM_EPS = 1e-6
SUBLN_EPS = 1e-5
NEG = -1e30
PAGE_SIZE = 128
LAM_INIT = 0.8 - 0.6 * math.exp(-0.3 * 0)
SCALE = HEAD_DIM ** -0.5
QSCALE = SCALE * math.log2(math.e)

LANES = 128
TOK_TILE = 256
ATT_TILE = 256
ATT_PAIRS = 4
VT_ROWS = LANES + 16
PAGES_PER_STEP = 16
PAGE_SLOTS = 4
N_GROUPS = 8
VMEM_LIMIT = 56 * 1024 * 1024

_OFF_QA, _OFF_KA, _OFF_VA, _OFF_QB, _OFF_KB, _OFF_VB, _OFF_G, _OFF_END = (
    0, 512, 1024, 1536, 2048, 2560, 3072, 5120)


def _const_spec(shape):
    nd = len(shape)
    return pl.BlockSpec(shape, lambda *_: (0,) * nd, pipeline_mode=pl.Buffered(1))


def _rope_tables(pos):
    half = ROT_DIM // 2
    inv = ROPE_THETA ** (-jnp.arange(half, dtype=F32) * 2.0 / ROT_DIM)
    lane = np.arange(LANES) % HEAD_DIM
    rot = lane < ROT_DIM
    first = lane < half
    inv_full = jnp.where(rot, inv[lane % half], 0.0)
    ang = pos.astype(F32)[:, None] * inv_full[None, :]
    sin = jnp.sin(ang)
    return (jnp.cos(ang), jnp.where(first, -sin, 0.0), jnp.where(rot & ~first, sin, 0.0))


def _inproj_kernel(x_ref, g_ref, w_ref, bg_ref, c_ref, s1_ref, s2_ref,
                   ka_ref, va_ref, kb_ref, vb_ref,
                   qa_ref, kaug_ref, vab_ref, qb_ref, kbb_ref, vbb_ref,
                   gate_ref, kmean_ref, *, blocks_per_seq, device_layout):
    x = x_ref[...]
    ms = jnp.mean(x * x, axis=-1, keepdims=True)
    xn = (x * lax.rsqrt(ms + NORM_EPS) * g_ref[...]).astype(BF16)
    c, s1, s2 = c_ref[...], s1_ref[...], s2_ref[...]

    def proj(lo, hi):
        return jnp.dot(xn, w_ref[:, lo:hi], preferred_element_type=F32)

    def rope(z):
        outs = []
        for k in range(z.shape[1] // LANES):
            zc = z[:, k * LANES:(k + 1) * LANES]
            outs.append(zc * c + pltpu.roll(zc, LANES - ROT_DIM // 2, 1) * s1
                        + pltpu.roll(zc, ROT_DIM // 2, 1) * s2)
        return jnp.concatenate(outs, axis=1)

    qa = rope(proj(_OFF_QA, _OFF_KA))
    qa_ref[...] = (qa * QSCALE).astype(BF16)

    def store_kv(ref, val, val_t=None):
        if device_layout:
            ref[...] = val.T if val_t is None else val_t
        else:
            ref[...] = val

    def store_vt(ref, vt):
        for p in range(WIDTH // LANES):
            ref[0, p, 0:LANES, :] = vt[p * LANES:(p + 1) * LANES].astype(BF16)
            ref[0, p, LANES:VT_ROWS, :] = jnp.ones((VT_ROWS - LANES, vt.shape[1]), BF16)

    ka = rope(proj(_OFF_KA, _OFF_VA))
    store_kv(ka_ref, ka)
    kmean_ref[0] = jnp.mean(ka, axis=0, keepdims=True)
    blk = pl.program_id(0) % blocks_per_seq
    lane = lax.broadcasted_iota(jnp.int32, (x.shape[0], LANES), 1)
    onehot = jnp.where((lane == blk) | (lane == blk + 32), 1.0, 0.0).astype(BF16)
    ka_bf = ka.astype(BF16)
    for p in range(WIDTH // LANES):
        kaug_ref[:, 2 * p * LANES:(2 * p + 1) * LANES] = ka_bf[:, p * LANES:(p + 1) * LANES]
        kaug_ref[:, (2 * p + 1) * LANES:(2 * p + 2) * LANES] = onehot

    va = proj(_OFF_VA, _OFF_QB)
    va_t = va.T
    store_kv(va_ref, va, va_t)
    store_vt(vab_ref, va_t)

    qb = rope(proj(_OFF_QB, _OFF_KB))
    qb_ref[...] = (qb * QSCALE).astype(BF16)

    kb = rope(proj(_OFF_KB, _OFF_VB))
    store_kv(kb_ref, kb)
    kbb_ref[...] = kb.astype(BF16)

    vb = proj(_OFF_VB, _OFF_G)
    if device_layout:
        for h in range(B_HEADS):
            vb_ref[pl.ds(h, x.shape[0], stride=B_HEADS), :] = vb[:, h * LANES:(h + 1) * LANES]
    else:
        vb_ref[...] = vb
    store_vt(vbb_ref, vb.T)

    z = proj(_OFF_G, _OFF_END) + bg_ref[...]
    gate_ref[...] = 1.0 / (1.0 + jnp.exp(-z))


def _inproj(x, norm_g, w_in, b_gate, tables, blocks_per_seq, device_layout):
    t = x.shape[0]
    nt = t // TOK_TILE
    tab_tiles = tables[0].shape[0] // TOK_TILE
    row = lambda w: pl.BlockSpec((TOK_TILE, w), lambda i: (i, 0))
    tab = pl.BlockSpec((TOK_TILE, LANES), lambda i: (i % tab_tiles, 0))
    bf16o = jax.ShapeDtypeStruct((t, WIDTH), BF16)
    vt_spec = pl.BlockSpec((1, WIDTH // LANES, VT_ROWS, TOK_TILE), lambda i: (i, 0, 0, 0))
    vt_shape = jax.ShapeDtypeStruct((nt, WIDTH // LANES, VT_ROWS, TOK_TILE), BF16)
    if device_layout:
        nb = nt // blocks_per_seq
        kv_spec = pl.BlockSpec((None, WIDTH, TOK_TILE),
                               lambda i: (i // blocks_per_seq, 0, i % blocks_per_seq))
        kv_shape = jax.ShapeDtypeStruct((nb, WIDTH, blocks_per_seq * TOK_TILE), F32)
        f32_specs = [kv_spec] * 3 + [pl.BlockSpec((TOK_TILE * B_HEADS, LANES), lambda i: (i, 0))]
        f32_shapes = [kv_shape] * 3 + [jax.ShapeDtypeStruct((t * B_HEADS, LANES), F32)]
    else:
        f32_specs = [row(WIDTH)] * 4
        f32_shapes = [jax.ShapeDtypeStruct((t, WIDTH), F32)] * 4
    return pl.pallas_call(
        functools.partial(_inproj_kernel, blocks_per_seq=blocks_per_seq,
                          device_layout=device_layout),
        grid=(nt,),
        in_specs=[row(D_MODEL), _const_spec((1, D_MODEL)), _const_spec((D_MODEL, _OFF_END)),
                  _const_spec((1, 2 * D_MODEL)), tab, tab, tab],
        out_specs=f32_specs + [row(WIDTH), row(2 * WIDTH), vt_spec, row(WIDTH), row(WIDTH), vt_spec]
                  + [row(2 * D_MODEL), pl.BlockSpec((1, 1, WIDTH), lambda i: (i, 0, 0))],
        out_shape=f32_shapes + [bf16o, jax.ShapeDtypeStruct((t, 2 * WIDTH), BF16), vt_shape, bf16o,
                                bf16o, vt_shape]
                  + [jax.ShapeDtypeStruct((t, 2 * D_MODEL), F32),
                     jax.ShapeDtypeStruct((nt, 1, WIDTH), F32)],
        compiler_params=pltpu.CompilerParams(dimension_semantics=("parallel",),
                                             vmem_limit_bytes=VMEM_LIMIT),
        name="inproj",
    )(x, norm_g, w_in, b_gate, *tables)


def _top3_bias(g, valid, forced, seg_masks, idx, axis):
    selected = forced
    big = jnp.float32(1 << 20)
    for seg in seg_masks:
        gh = jnp.where(seg, jnp.where(valid, g, NEG), -jnp.inf)
        for _ in range(MOBA_TOPK):
            mx = jnp.max(gh, axis=axis, keepdims=True)
            first = jnp.min(jnp.where(gh == mx, idx, big), axis=axis, keepdims=True)
            pick = idx == first
            selected = selected | (pick & valid)
            gh = jnp.where(pick, -jnp.inf, gh)
    return jnp.where(selected, 0.0, NEG)


def _pair_attn_kernel(*refs, moba, pairs):
    qs_ref, s_ref, p_ref, acc_ref, m_ref, c_ref, a_ref = refs[-7:]
    if moba:
        q_ref, k_ref, v_ref, mhi_ref, mlo_ref, o_ref = refs[:-7]
    else:
        q_ref, k_ref, v_ref, lq1_ref, lk1_ref, lq2_ref, lk2_ref, sg_ref, o_ref = refs[:-7]
    i = pl.program_id(2)
    tq = ATT_TILE
    kw = k_ref.shape[1] // pairs
    chains = range(2 * pairs)
    lane = lax.broadcasted_iota(jnp.int32, (tq, LANES), 1)
    for pr in range(pairs):
        q = q_ref[:, pr * LANES:(pr + 1) * LANES]
        zero = jnp.zeros_like(q)
        qs_ref[2 * pr, :, 0:LANES] = jnp.where(lane < HEAD_DIM, q, zero)
        qs_ref[2 * pr + 1, :, 0:LANES] = jnp.where(lane >= HEAD_DIM, q, zero)
        if moba:
            contract_last = (((1,), (1,)), ((), ()))
            g = (lax.dot_general(mhi_ref[pr], q, contract_last, preferred_element_type=F32)
                 + lax.dot_general(mlo_ref[pr], q, contract_last, preferred_element_type=F32))
            ridx = lax.broadcasted_iota(jnp.int32, g.shape, 0)
            blk = ridx & 31
            bias_t = _top3_bias(g, blk < i, blk == i, [ridx < 32, ridx >= 32], ridx.astype(F32), 0)
            bias = jnp.concatenate([bias_t, jnp.zeros_like(bias_t)], axis=0).T.astype(BF16)
            zb = jnp.zeros_like(bias)
            qs_ref[2 * pr, :, LANES:2 * LANES] = jnp.where(lane < 32, bias, zb)
            qs_ref[2 * pr + 1, :, LANES:2 * LANES] = jnp.where((lane >= 32) & (lane < 64), bias, zb)

    def scores(j, ch):
        start = pl.multiple_of(j * ATT_TILE, ATT_TILE)
        kb = k_ref[pl.ds(start, ATT_TILE), (ch // 2) * kw:(ch // 2 + 1) * kw]
        st = lax.dot_general(kb, qs_ref[ch], (((1,), (1,)), ((), ())),
                             preferred_element_type=F32)
        s_ref[ch] = st
        c_ref[ch] = jnp.max(st, axis=0, keepdims=True)

    def softmax(ch, diagonal):
        st = s_ref[ch]
        if diagonal:
            kidx = lax.broadcasted_iota(jnp.int32, (ATT_TILE, tq), 0)
            qidx = lax.broadcasted_iota(jnp.int32, (ATT_TILE, tq), 1)
            st = jnp.where(kidx <= qidx, st, NEG)
            cmax = jnp.max(st, axis=0, keepdims=True)
        else:
            cmax = c_ref[ch]
        m = m_ref[ch]
        m_new = jnp.maximum(m, cmax)
        p_ref[ch] = jnp.exp2(st - m_new).astype(BF16)
        m_ref[ch] = m_new
        a_ref[ch] = jnp.exp2(m - m_new)

    def value(jv, ch):
        acc_ref[ch] = a_ref[ch] * acc_ref[ch] + jnp.dot(v_ref[jv, ch // 2], p_ref[ch],
                                                        preferred_element_type=F32)

    def body(j, carry):
        for ch in chains:
            value(jnp.maximum(j - 1, 0), ch)
            softmax(ch, False)
            scores(j + 1, ch)
        return carry

    for ch in chains:
        scores(0, ch)
        p_ref[ch] = jnp.zeros((ATT_TILE, tq), BF16)
        acc_ref[ch] = jnp.zeros((VT_ROWS, tq), F32)
        m_ref[ch] = jnp.full((1, tq), -jnp.inf, F32)
        a_ref[ch] = jnp.ones((1, tq), F32)
    lax.fori_loop(0, i, body, 0)
    if not moba:
        lam = (jnp.exp(jnp.sum(lq1_ref[...] * lk1_ref[...], axis=1, keepdims=True))
               - jnp.exp(jnp.sum(lq2_ref[...] * lk2_ref[...], axis=1, keepdims=True)) + LAM_INIT)
    for pr in range(pairs):
        ots = []
        for ch in (2 * pr, 2 * pr + 1):
            value(jnp.maximum(i - 1, 0), ch)
            softmax(ch, True)
            value(i, ch)
            acc = acc_ref[ch]
            ots.append(acc[0:LANES] * (1.0 / acc[LANES:LANES + 1]))
        ot0, ot1 = ots
        if moba:
            ot = jnp.concatenate([ot0[0:HEAD_DIM], ot1[HEAD_DIM:LANES]], axis=0)
            o_ref[:, pr * LANES:(pr + 1) * LANES] = ot.T.astype(o_ref.dtype)
        else:
            d = (ot0 - lam * ot1).T
            y = d * lax.rsqrt(jnp.mean(d * d, axis=-1, keepdims=True) + SUBLN_EPS) * sg_ref[...]
            o_ref[:, pr * LANES:(pr + 1) * LANES] = (y * (1.0 - LAM_INIT)).astype(o_ref.dtype)


def _pair_attn(q, k, v, extras, moba):
    b, s, _ = q.shape
    npair = WIDTH // LANES
    kw = k.shape[2] // npair
    nblk = s // ATT_TILE
    pairs = ATT_PAIRS
    nch = 2 * pairs
    in_specs = [pl.BlockSpec((None, ATT_TILE, pairs * LANES), lambda bi, g, i: (bi, i, g)),
                pl.BlockSpec((None, s, pairs * kw), lambda bi, g, i: (bi, 0, g),
                             pipeline_mode=pl.Buffered(1)),
                pl.BlockSpec((None, nblk, pairs, VT_ROWS, ATT_TILE), lambda bi, g, i: (bi, 0, g, 0, 0),
                             pipeline_mode=pl.Buffered(1))]
    if moba:
        in_specs += [pl.BlockSpec((None, pairs, 2 * 32, LANES), lambda bi, g, i: (bi, g, 0, 0))] * 2
    else:
        in_specs += [_const_spec((1, HEAD_DIM))] * 4 + [_const_spec((1, LANES))]
    stat = pltpu.VMEM((nch, 1, ATT_TILE), F32)
    return pl.pallas_call(
        functools.partial(_pair_attn_kernel, moba=moba, pairs=pairs),
        grid=(b, npair // pairs, s // ATT_TILE),
        in_specs=in_specs,
        out_specs=pl.BlockSpec((None, ATT_TILE, pairs * LANES), lambda bi, g, i: (bi, i, g)),
        out_shape=jax.ShapeDtypeStruct((b, s, WIDTH), BF16),
        scratch_shapes=[pltpu.VMEM((nch, ATT_TILE, kw), BF16),
                        pltpu.VMEM((nch, ATT_TILE, ATT_TILE), F32),
                        pltpu.VMEM((nch, ATT_TILE, ATT_TILE), BF16),
                        pltpu.VMEM((nch, VT_ROWS, ATT_TILE), F32),
                        stat, stat, stat],
        compiler_params=pltpu.CompilerParams(
            dimension_semantics=("parallel", "parallel", "arbitrary"),
            vmem_limit_bytes=VMEM_LIMIT),
        name="moba_prompt" if moba else "diff_prompt",
    )(q, k, v, *extras)


def _moba_mean_mats(kmean, b, s):
    nblk = s // MOBA_BLOCK
    mean = kmean.reshape(b, nblk, WIDTH // LANES, 2, HEAD_DIM)
    mt = jnp.transpose(mean, (0, 2, 3, 1, 4))
    z = jnp.zeros((b, WIDTH // LANES, 2, 32, HEAD_DIM), F32).at[:, :, :, :nblk].set(mt)
    eye = jnp.eye(2, dtype=F32)
    m = jnp.einsum('bphnd,hg->bphngd', z, eye).reshape(b, WIDTH // LANES, 2 * 32, LANES)
    hi = m.astype(BF16)
    lo = (m - hi.astype(F32)).astype(BF16)
    return hi, lo


def _decode_kernel(*refs, moba, n_pages, t_new):
    pg = PAGES_PER_STEP
    ns = n_pages // pg
    step_keys = pg * PAGE_SIZE
    ppb = MOBA_BLOCK // PAGE_SIZE
    bps = pg // ppb
    ncol = N_GROUPS * t_new
    pt_ref, qbd_ref, knew_ref, vnew_ref = refs[:4]
    pos = 4
    if not moba:
        lq1_ref, lk1_ref, lq2_ref, lk2_ref, sg_ref = refs[pos:pos + 5]
        pos += 5
    ck_ref, cv_ref, o_ref, ring_ref, sem_ref = refs[pos:pos + 5]
    sc_ref, kstage_ref, vstage_ref, acc_ref, onew_ref, m_ref, il_ref = refs[pos + 5:pos + 12]
    if moba:
        mean_ref, = refs[pos + 12:pos + 13]
    b = pl.program_id(0)
    s = pl.program_id(1)
    lane = lax.broadcasted_iota(jnp.int32, (ncol, LANES), 1)
    row = lax.broadcasted_iota(jnp.int32, (ncol, LANES), 0)

    steps = 2 * ns
    g = b * steps + s
    total = pl.num_programs(0) * steps

    def page_copy(cache_ref, page, slot, k):
        return pltpu.make_async_copy(cache_ref.at[page], ring_ref.at[slot, k], sem_ref.at[slot])

    def start_batch(gb):
        bb = gb // steps
        sb = gb - bb * steps
        slot = gb % PAGE_SLOTS

        @pl.when(sb < ns)
        def _():
            for k in range(pg):
                page_copy(ck_ref, pt_ref[bb, sb * pg + k], slot, k).start()

        @pl.when(sb >= ns)
        def _():
            for k in range(pg):
                page_copy(cv_ref, pt_ref[bb, (sb - ns) * pg + k], slot, k).start()

    @pl.when(g == 0)
    def _prime():
        for ahead in range(PAGE_SLOTS - 1):
            start_batch(g + ahead)

    @pl.when(g + PAGE_SLOTS - 1 < total)
    def _prefetch():
        start_batch(g + PAGE_SLOTS - 1)

    slot = g % PAGE_SLOTS
    for k in range(pg):
        page_copy(ck_ref, 0, slot, k).wait()
    kpages = [ring_ref.at[slot, k] for k in range(pg)]
    vpages = kpages

    if moba:
        @pl.when(s == 0)
        def _init():
            mean_ref[...] = jnp.zeros_like(mean_ref)

    @pl.when(s < ns)
    def _key_phase():
        for k in range(pg):
            kp = kpages[k][...]
            kstage_ref[:, k * PAGE_SIZE:(k + 1) * PAGE_SIZE] = kp.astype(BF16)
            if moba and k % ppb == ppb - 1:
                tot = kp
                for kk in range(1, ppb):
                    tot = tot + kpages[k - kk][...]
                mean = jnp.sum(tot, axis=1, keepdims=True) * (1.0 / MOBA_BLOCK)
                n = s * bps + k // ppb
                lane_m = lax.broadcasted_iota(jnp.int32, (WIDTH, LANES), 1)
                mean_ref[...] = jnp.where(lane_m == n, mean, mean_ref[...])
        sc_ref[s] = jnp.dot(qbd_ref[...], kstage_ref[...], preferred_element_type=F32)

    @pl.when(s == ns)
    def _softmax_stats():
        pad = PAGE_SIZE - t_new
        kn = jnp.concatenate([knew_ref[...], jnp.zeros((pad, WIDTH), F32)], axis=0).astype(BF16)
        vn = jnp.concatenate([vnew_ref[...], jnp.zeros((pad, WIDTH), F32)], axis=0).astype(BF16)
        sn = lax.dot_general(qbd_ref[...], kn, (((1,), (1,)), ((), ())),
                             preferred_element_type=F32)
        sn = jnp.where((lane < t_new) & (lane <= (row & (t_new - 1))), sn, NEG)
        if moba:
            mean = mean_ref[...]
            mhi = mean.astype(BF16)
            mlo = (mean - mhi.astype(F32)).astype(BF16)
            g = (jnp.dot(qbd_ref[...], mhi, preferred_element_type=F32)
                 + jnp.dot(qbd_ref[...], mlo, preferred_element_type=F32))
            valid = lane < n_pages // ppb
            bias = _top3_bias(g, valid, lane < 0, [lane >= 0], lane.astype(F32), 1)

            def add_bias(st, carry):
                cols = []
                for bi in range(bps):
                    col = jnp.sum(jnp.where(lane == st * bps + bi, bias, 0.0), axis=1, keepdims=True)
                    cols.append(jnp.broadcast_to(col, (ncol, MOBA_BLOCK)))
                sc_ref[st] = sc_ref[st] + jnp.concatenate(cols, axis=1)
                return carry

            lax.fori_loop(0, ns, add_bias, 0)

        def max_body(st, m):
            return jnp.maximum(m, jnp.max(sc_ref[st], axis=1, keepdims=True))

        m = lax.fori_loop(0, ns, max_body, jnp.max(sn, axis=1, keepdims=True))

        def sum_body(st, l):
            return l + jnp.sum(jnp.exp2(sc_ref[st] - m), axis=1, keepdims=True)

        pn = jnp.exp2(sn - m)
        l = lax.fori_loop(0, ns, sum_body, jnp.sum(pn, axis=1, keepdims=True))
        il = 1.0 / l
        m_ref[...] = m
        il_ref[...] = il
        pn = (pn * il).astype(BF16)
        if moba:
            onew_ref[...] = jnp.dot(pn, vn, preferred_element_type=F32)
            acc_ref[...] = jnp.zeros_like(acc_ref)
        else:
            for h in range(B_HEADS):
                onew_ref[h] = jnp.dot(pn[2 * h * t_new:(2 * h + 2) * t_new],
                                      vn[:, h * LANES:(h + 1) * LANES], preferred_element_type=F32)
            acc_ref[...] = jnp.zeros_like(acc_ref)

    @pl.when(s >= ns)
    def _value_phase():
        p = (jnp.exp2(sc_ref[s - ns] - m_ref[...]) * il_ref[...]).astype(BF16)
        if moba:
            for k in range(pg):
                vstage_ref[:, k * PAGE_SIZE:(k + 1) * PAGE_SIZE] = vpages[k][...].astype(BF16)
            acc_ref[...] += lax.dot_general(p, vstage_ref[...], (((1,), (1,)), ((), ())),
                                            preferred_element_type=F32)
        else:
            for k in range(pg):
                for h in range(B_HEADS):
                    vh = vpages[k][pl.ds(h, PAGE_SIZE, stride=B_HEADS), :]
                    vstage_ref[h, k * PAGE_SIZE:(k + 1) * PAGE_SIZE, :] = vh.astype(BF16)
            for h in range(B_HEADS):
                acc_ref[h] += jnp.dot(p[2 * h * t_new:(2 * h + 2) * t_new], vstage_ref[h],
                                      preferred_element_type=F32)

    @pl.when(s == 2 * ns - 1)
    def _finalize():
        if moba:
            tot = acc_ref[...] + onew_ref[...]
            lane_o = lax.broadcasted_iota(jnp.int32, (t_new, WIDTH), 1)
            out = jnp.zeros((t_new, WIDTH), F32)
            for gidx in range(N_GROUPS):
                out = jnp.where(lane_o // HEAD_DIM == gidx, tot[gidx * t_new:(gidx + 1) * t_new], out)
            o_ref[...] = out
        else:
            lam = (jnp.exp(jnp.sum(lq1_ref[...] * lk1_ref[...], axis=1, keepdims=True))
                   - jnp.exp(jnp.sum(lq2_ref[...] * lk2_ref[...], axis=1, keepdims=True))
                   + LAM_INIT)
            for h in range(B_HEADS):
                tot = acc_ref[h] + onew_ref[h]
                d = tot[0:t_new] - lam * tot[t_new:2 * t_new]
                y = d * lax.rsqrt(jnp.mean(d * d, axis=-1, keepdims=True) + SUBLN_EPS) * sg_ref[...]
                o_ref[:, h * LANES:(h + 1) * LANES] = y * (1.0 - LAM_INIT)


def _decode_attn(page_table, qbd, k_new, v_new, cache_k, cache_v, extras, moba):
    db, n_pages = page_table.shape
    t_new = k_new.shape[1]
    pg = PAGES_PER_STEP
    ns = n_pages // pg

    ncol = N_GROUPS * t_new
    step_keys = pg * PAGE_SIZE
    per_seq = lambda shape: pl.BlockSpec((None,) + shape, lambda b, s, pt: (b, 0, 0))
    const = lambda shape: pl.BlockSpec(shape, lambda b, s, pt: (0, 0))
    in_specs = [per_seq((ncol, WIDTH)), per_seq((t_new, WIDTH)), per_seq((t_new, WIDTH))]
    if not moba:
        in_specs += [const((1, HEAD_DIM))] * 4 + [const((1, LANES))]
    in_specs += [pl.BlockSpec(memory_space=pl.ANY)] * 2
    scratch = [pltpu.VMEM((PAGE_SLOTS, pg, WIDTH, LANES), F32),
               pltpu.SemaphoreType.DMA((PAGE_SLOTS,)),
               pltpu.VMEM((ns, ncol, step_keys), F32),
               pltpu.VMEM((WIDTH, step_keys), BF16)]
    if moba:
        scratch += [pltpu.VMEM((WIDTH, step_keys), BF16),
                    pltpu.VMEM((ncol, WIDTH), F32), pltpu.VMEM((ncol, WIDTH), F32)]
    else:
        scratch += [pltpu.VMEM((B_HEADS, step_keys, LANES), BF16),
                    pltpu.VMEM((B_HEADS, 2 * t_new, LANES), F32),
                    pltpu.VMEM((B_HEADS, 2 * t_new, LANES), F32)]
    scratch += [pltpu.VMEM((ncol, 1), F32), pltpu.VMEM((ncol, 1), F32)]
    if moba:
        scratch += [pltpu.VMEM((WIDTH, LANES), F32)]
    grid_spec = pltpu.PrefetchScalarGridSpec(
        num_scalar_prefetch=1, grid=(db, 2 * ns), in_specs=in_specs,
        out_specs=pl.BlockSpec((None, t_new, WIDTH), lambda b, s, pt: (b, 0, 0)),
        scratch_shapes=scratch)
    return pl.pallas_call(
        functools.partial(_decode_kernel, moba=moba, n_pages=n_pages, t_new=t_new),
        grid_spec=grid_spec,
        out_shape=jax.ShapeDtypeStruct((db, t_new, WIDTH), F32),
        compiler_params=pltpu.CompilerParams(dimension_semantics=("arbitrary", "arbitrary"),
                                             vmem_limit_bytes=VMEM_LIMIT),
        name="moba_decode" if moba else "diff_decode",
    )(page_table, qbd, k_new, v_new, *extras, cache_k, cache_v)


def _block_diag_queries(q, db, t_new):
    q4 = q.reshape(db, t_new, N_GROUPS, HEAD_DIM)
    eye = jnp.eye(N_GROUPS, dtype=q.dtype)
    return jnp.einsum('btgd,hg->bhtgd', q4, eye).reshape(db, N_GROUPS * t_new, WIDTH)


def _pages_feature_major(cache):
    nd = cache.ndim
    moved = jnp.transpose(cache, (0,) + tuple(range(2, nd)) + (1,))
    return moved.reshape(cache.shape[0], WIDTH, cache.shape[1])


def _merge_mlp_kernel(x_ref, oa_ref, ob_ref, gate_ref, wpa_ref, wpb_ref, wo_ref, gm_ref,
                      wup_ref, wdn_ref, gf_ref, y_ref):
    ga = gate_ref[:, 0:D_MODEL]
    gb = gate_ref[:, D_MODEL:2 * D_MODEL]
    t = (ga * jnp.dot(oa_ref[...], wpa_ref[...], preferred_element_type=F32)
         + gb * jnp.dot(ob_ref[...], wpb_ref[...], preferred_element_type=F32))
    h = x_ref[...] + jnp.dot(t.astype(BF16), wo_ref[...], preferred_element_type=F32)
    hn = (h * lax.rsqrt(jnp.mean(h * h, axis=-1, keepdims=True) + NORM_EPS) * gm_ref[...]).astype(BF16)
    y = h
    chunk = D_FF // 4
    for c in range(D_FF // chunk):
        u = jnp.dot(hn, wup_ref[:, c * chunk:(c + 1) * chunk], preferred_element_type=F32)
        a = jnp.square(jnp.maximum(u, 0.0)).astype(BF16)
        y = y + jnp.dot(a, wdn_ref[c * chunk:(c + 1) * chunk, :], preferred_element_type=F32)
    y_ref[...] = y * lax.rsqrt(jnp.mean(y * y, axis=-1, keepdims=True) + NORM_EPS) * gf_ref[...]


def _merge_mlp(x, oa, ob, gates, w_pa, w_pb, w_o, norm_mlp, w_up, w_down, norm_final):
    t = x.shape[0]
    row = lambda w: pl.BlockSpec((TOK_TILE, w), lambda i: (i, 0))
    return pl.pallas_call(
        _merge_mlp_kernel,
        grid=(t // TOK_TILE,),
        in_specs=[row(D_MODEL), row(WIDTH), row(WIDTH), row(2 * D_MODEL),
                  _const_spec((WIDTH, D_MODEL)), _const_spec((WIDTH, D_MODEL)),
                  _const_spec((D_MODEL, D_MODEL)), _const_spec((1, D_MODEL)),
                  _const_spec((D_MODEL, D_FF)), _const_spec((D_FF, D_MODEL)),
                  _const_spec((1, D_MODEL))],
        out_specs=row(D_MODEL),
        out_shape=jax.ShapeDtypeStruct((t, D_MODEL), F32),
        compiler_params=pltpu.CompilerParams(dimension_semantics=("parallel",),
                                             vmem_limit_bytes=VMEM_LIMIT),
        name="merge_mlp",
    )(x, oa, ob, gates, w_pa, w_pb, w_o, norm_mlp, w_up, w_down, norm_final)


def kernel(x_prompt, x_sample, cache_moba_k, cache_moba_v, cache_diff_k, cache_diff_v, page_table,
           norm_attn, w_in, b_gate, lambda_q1, lambda_k1, lambda_q2, lambda_k2, subln_g,
           w_pa, w_pb, w_o, norm_mlp, w_up, w_down, norm_final):
    b, s, _ = x_prompt.shape
    db, t_new, _ = x_sample.shape
    n_pool = cache_moba_k.shape[0]
    past_len = page_table.shape[1] * PAGE_SIZE
    assert norm_attn.shape[0] == 1 and s % MOBA_BLOCK == 0 and s // MOBA_BLOCK <= 32
    assert (db * t_new) % TOK_TILE == 0 and past_len % MOBA_BLOCK == 0 and t_new == 8
    assert page_table.shape[1] % PAGES_PER_STEP == 0

    w_in_bf = w_in[0, :, :_OFF_END].astype(BF16)
    ng = norm_attn[0][None]
    bg = b_gate[0][None]
    lam_vecs = [v[0][None] for v in (lambda_q1, lambda_k1, lambda_q2, lambda_k2)]
    sg = subln_g[0][None]
    merge_w = (w_pa[0].astype(BF16), w_pb[0].astype(BF16), w_o[0].astype(BF16), norm_mlp[0][None],
               w_up[0].astype(BF16), w_down[0].astype(BF16), norm_final[None])

    xp = x_prompt.reshape(b * s, D_MODEL)
    tabs_p = _rope_tables(jnp.arange(s, dtype=jnp.int32))
    (ka, va, kb, vb, qa_bf, kaug, va_bf, qb_bf, kb_bf, vb_bf, gates_p, kmean) = _inproj(
        xp, ng, w_in_bf, bg, tabs_p, s // MOBA_BLOCK, device_layout=True)
    seq_minor = lambda a, dims: jnp.moveaxis(a.reshape((b, 1) + dims + (s,)), -1, 1)
    mhi, mlo = _moba_mean_mats(kmean, b, s)
    vt_view = lambda vt: vt.reshape(b, s // ATT_TILE, WIDTH // LANES, VT_ROWS, ATT_TILE)
    oa_p = _pair_attn(qa_bf.reshape(b, s, WIDTH), kaug.reshape(b, s, 2 * WIDTH),
                      vt_view(va_bf), (mhi, mlo), moba=True)
    ob_p = _pair_attn(qb_bf.reshape(b, s, WIDTH), kb_bf.reshape(b, s, WIDTH),
                      vt_view(vb_bf), (*lam_vecs, sg), moba=False)
    y_p = _merge_mlp(xp, oa_p.reshape(b * s, WIDTH), ob_p.reshape(b * s, WIDTH), gates_p, *merge_w)

    xs = x_sample.reshape(db * t_new, D_MODEL)
    tabs_s = tuple(jnp.tile(a, (db, 1)) for a in
                   _rope_tables(past_len + jnp.arange(t_new, dtype=jnp.int32)))
    (kas, vas, kbs, vbs, qas_bf, _, _, qbs_bf, _, _, gates_s, _) = _inproj(
        xs, ng, w_in_bf, bg, tabs_s, 1, device_layout=False)
    oa_s = _decode_attn(page_table, _block_diag_queries(qas_bf, db, t_new),
                        kas.reshape(db, t_new, WIDTH), vas.reshape(db, t_new, WIDTH),
                        _pages_feature_major(cache_moba_k), _pages_feature_major(cache_moba_v),
                        (), moba=True)
    ob_s = _decode_attn(page_table, _block_diag_queries(qbs_bf, db, t_new),
                        kbs.reshape(db, t_new, WIDTH), vbs.reshape(db, t_new, WIDTH),
                        _pages_feature_major(cache_diff_k),
                        cache_diff_v.reshape(n_pool, PAGE_SIZE * B_HEADS, LANES),
                        (*lam_vecs, sg), moba=False)
    y_s = _merge_mlp(xs, oa_s.reshape(db * t_new, WIDTH).astype(BF16),
                     ob_s.reshape(db * t_new, WIDTH).astype(BF16), gates_s, *merge_w)

    return (y_p.reshape(b, s, D_MODEL), y_s.reshape(db, t_new, D_MODEL),
            seq_minor(ka, (A_HEADS, HEAD_DIM)), seq_minor(va, (A_HEADS, HEAD_DIM)),
            seq_minor(kb, (B_HEADS, 2, HEAD_DIM)), vb.reshape(b, s, 1, B_HEADS, 2 * HEAD_DIM),
            kas.reshape(db, t_new, 1, A_HEADS, HEAD_DIM), vas.reshape(db, t_new, 1, A_HEADS, HEAD_DIM),
            kbs.reshape(db, t_new, 1, B_HEADS, 2, HEAD_DIM),
            vbs.reshape(db, t_new, 1, B_HEADS, 2 * HEAD_DIM))
```
